```python
import jax
import jax.numpy as jnp
from jax import lax
import numpy as np

D_MODEL = 1024
BATCH = 2
SEQ = 8192
DEPTH = 2

D_MIX = D_MODEL
HG_HEADS = 4
HG_DK = 64
HG_DV = 64
HG_WIDTH = HG_HEADS * HG_DV
HG_CHUNK = 64
POOL_WINDOWS = (2, 4, 8, 16)
POOL_GROUPS = len(POOL_WINDOWS)
POOL_GDIM = 64
POOL_WIDTH = POOL_GROUPS * POOL_GDIM
ATT_HEADS = 8
ATT_KV_HEADS = 2
ATT_HDIM = 64
ATT_WIDTH = ATT_HEADS * ATT_HDIM
WINDOW = 128
ATT_BLOCK = WINDOW
D_FF = 2816
CONV_WIDTH = 3
EPS = 1e-6
IN_COLS = (HG_HEADS * HG_DK, HG_HEADS * HG_DK, HG_WIDTH, HG_WIDTH, POOL_WIDTH,
           ATT_WIDTH, ATT_KV_HEADS * ATT_HDIM, ATT_KV_HEADS * ATT_HDIM)
IN_SPLITS = tuple(int(s) for s in np.cumsum(IN_COLS)[:-1])
D_IN = sum(IN_COLS)

kernel_name = 'hymba_style_hybrid_block'


def rms_norm(x, w):
    xf = x.astype(jnp.float32)
    y = xf * lax.rsqrt(jnp.mean(xf * xf, axis=-1, keepdims=True) + EPS)
    return (y * w.astype(jnp.float32)).astype(x.dtype)


def hgrn2_mixer(q, f_logit, i, gate, lb, norm_w):
    B, S, _ = q.shape
    dt = q.dtype
    f32 = jnp.float32
    qf = jax.nn.silu(q.astype(f32)) * HG_DK ** -0.5
    lb = lb.astype(f32)
    log_f = jnp.logaddexp(jnp.log(lb), jnp.log1p(-lb) + jax.nn.log_sigmoid(f_logit.astype(f32)))
    k = -jnp.expm1(log_f)
    n = S // HG_CHUNK

    def to_chunks(t, d):
        return t.reshape(B, n, HG_CHUNK, HG_HEADS, d).transpose(1, 0, 3, 2, 4)

    qc, kc, gc = to_chunks(qf, HG_DK), to_chunks(k, HG_DK), to_chunks(log_f, HG_DK)
    vc = to_chunks(i.astype(f32), HG_DV)
    causal = jnp.tril(jnp.ones((HG_CHUNK, HG_CHUNK), bool))[:, :, None]

    def step(state, inp):
        q_, k_, v_, g_ = inp
        b = jnp.cumsum(g_, axis=2)
        o_inter = jnp.einsum('bhck,bhkv->bhcv', q_ * jnp.exp(b), state)
        diff = b[:, :, :, None, :] - b[:, :, None, :, :]
        decay = jnp.where(causal, jnp.exp(jnp.where(causal, diff, 0.0)), 0.0)
        scores = jnp.einsum('bhik,bhjk,bhijk->bhij', q_, k_, decay)
        o = o_inter + jnp.einsum('bhij,bhjv->bhiv', scores, v_)
        b_last = b[:, :, -1:, :]
        state = (jnp.exp(b_last[:, :, 0, :])[..., None] * state
                 + jnp.einsum('bhck,bhcv->bhkv', k_ * jnp.exp(b_last - b), v_))
        return state, o

    s0 = jnp.zeros((B, HG_HEADS, HG_DK, HG_DV), f32)
    _, o = lax.scan(step, s0, (qc, kc, vc, gc))
    o = o.transpose(1, 0, 3, 2, 4).reshape(B, S, HG_HEADS, HG_DV)
    o = o * lax.rsqrt(jnp.mean(o * o, axis=-1, keepdims=True) + EPS) * norm_w.astype(f32)
    o = o * jax.nn.silu(gate.astype(f32).reshape(B, S, HG_HEADS, HG_DV))
    return o.reshape(B, S, HG_WIDTH).astype(dt)


def pool_mixer(v, w_grp, scale):
    B, S, _ = v.shape
    f32 = jnp.float32
    vf = v.astype(f32).reshape(B, S, POOL_GROUPS, POOL_GDIM)
    cs = jnp.pad(jnp.cumsum(vf, axis=1), ((0, 0), (1, 0), (0, 0), (0, 0)))
    t = jnp.arange(S)
    pooled = []
    for gi, w in enumerate(POOL_WINDOWS):
        lo = jnp.pad(cs[:, :S - w + 1, gi], ((0, 0), (w - 1, 0), (0, 0)))
        cnt = jnp.minimum(t + 1, w).astype(f32)[None, :, None]
        pooled.append((cs[:, 1:, gi] - lo) / cnt)
    pooled = jnp.stack(pooled, axis=2) - vf
    y = jnp.einsum('bsgc,gcd->bsgd', pooled, w_grp.astype(f32)).reshape(B, S, POOL_WIDTH)
    return (y * scale.astype(f32)).astype(v.dtype)


def sink_swa_attention(q, k, v, sinks):
    B, S, _ = q.shape
    f32 = jnp.float32
    nb = S // ATT_BLOCK
    G = ATT_HEADS // ATT_KV_HEADS
    qb = q.astype(f32).reshape(B, nb, ATT_BLOCK, ATT_KV_HEADS, G, ATT_HDIM)

    def band(t):
        t = t.astype(f32).reshape(B, S, ATT_KV_HEADS, ATT_HDIM)
        t = jnp.pad(t, ((0, 0), (ATT_BLOCK, 0), (0, 0), (0, 0)))
        t = t.reshape(B, nb + 1, ATT_BLOCK, ATT_KV_HEADS, ATT_HDIM)
        return jnp.concatenate([t[:, :-1], t[:, 1:]], axis=2)

    kb, vb = band(k), band(v)
    s = jnp.einsum('bnqhgd,bnkhd->bnhgqk', qb, kb) * ATT_HDIM ** -0.5
    qi = jnp.arange(ATT_BLOCK)[:, None] + ATT_BLOCK
    kj = jnp.arange(2 * ATT_BLOCK)[None, :]
    rel = qi - kj
    in_window = (rel >= 0) & (rel < WINDOW)
    blk = jnp.arange(nb)[:, None, None]
    valid = in_window[None] & ((blk > 0) | (kj[None] >= ATT_BLOCK))
    s = jnp.where(valid[None, :, None, None], s, -jnp.inf)
    sink = sinks.astype(f32).reshape(ATT_KV_HEADS, G)[None, None, :, :, None, None]
    m = jnp.maximum(jnp.max(s, axis=-1, keepdims=True), sink)
    p = jnp.exp(s - m)
    denom = jnp.sum(p, axis=-1, keepdims=True) + jnp.exp(sink - m)
    o = jnp.einsum('bnhgqk,bnkhd->bnqhgd', p / denom, vb)
    return o.reshape(B, S, ATT_WIDTH).astype(q.dtype)


def conv_ffn(h, w_up, conv_w, conv_b, w_down):
    u = h @ w_up
    u = lax.conv_general_dilated(u, conv_w[:, None, :], window_strides=(1,),
                                 padding=[(CONV_WIDTH - 1, 0)],
                                 dimension_numbers=('NWC', 'WIO', 'NWC'),
                                 feature_group_count=2 * D_FF) + conv_b
    gate, val = jnp.split(u, 2, axis=-1)
    return (jax.nn.gelu(gate, approximate=True) * val) @ w_down


def setup_inputs(seed: int = 0) -> dict:
    key = jax.random.key(seed)
    ks = jax.random.split(key, 20)

    def nrm(k, shape, scale):
        return jax.random.normal(k, shape, jnp.float32) * scale

    return {
        'x': nrm(ks[0], (BATCH, SEQ, D_MODEL), 1.0),
        'c': nrm(ks[1], (BATCH, D_MODEL), 1.0),
        'w_ada': nrm(ks[2], (DEPTH, D_MODEL, 6 * D_MODEL), D_MODEL ** -0.5),
        'b_ada': nrm(ks[3], (DEPTH, 6 * D_MODEL), 0.01),
        'g_pre_mix': 1.0 + nrm(ks[4], (DEPTH, D_MODEL), 0.1),
        'g_post_mix': 1.0 + nrm(ks[5], (DEPTH, D_MODEL), 0.1),
        'w_in': nrm(ks[6], (DEPTH, D_MODEL, D_IN), D_MODEL ** -0.5),
        'hg_lb_logits': nrm(ks[7], (DEPTH, HG_HEADS * HG_DK), 1.0),
        'hg_norm': 1.0 + nrm(ks[8], (DEPTH, HG_DV), 0.1),
        'pool_w': nrm(ks[9], (DEPTH, POOL_GROUPS, POOL_GDIM, POOL_GDIM), POOL_GDIM ** -0.5),
        'pool_scale': 1.0 + nrm(ks[10], (DEPTH, POOL_WIDTH), 0.1),
        'attn_sinks': nrm(ks[11], (DEPTH, ATT_HEADS), 1.0),
        'w_out': nrm(ks[12], (DEPTH, D_MIX, D_MODEL), D_MIX ** -0.5),
        'g_pre_ffn': 1.0 + nrm(ks[13], (DEPTH, D_MODEL), 0.1),
        'g_post_ffn': 1.0 + nrm(ks[14], (DEPTH, D_MODEL), 0.1),
        'w_up': nrm(ks[15], (DEPTH, D_MODEL, 2 * D_FF), D_MODEL ** -0.5),
        'conv_w': nrm(ks[16], (DEPTH, CONV_WIDTH, 2 * D_FF), CONV_WIDTH ** -0.5),
        'conv_b': nrm(ks[17], (DEPTH, 2 * D_FF), 0.01),
        'w_down': nrm(ks[18], (DEPTH, D_FF, D_MODEL), D_FF ** -0.5),
    }


def reference(x, c, w_ada, b_ada, g_pre_mix, g_post_mix, w_in, hg_lb_logits, hg_norm,
              pool_w, pool_scale, attn_sinks, w_out, g_pre_ffn, g_post_ffn, w_up,
              conv_w, conv_b, w_down):
    lb_all = jnp.cumsum(jax.nn.softmax(hg_lb_logits.astype(jnp.float32), axis=0), axis=0)
    lb_all = lb_all - lb_all[0:1]
    c_act = jax.nn.silu(c)
    for l in range(DEPTH):
        mod = c_act @ w_ada[l] + b_ada[l]
        sh1, sc1, gt1, sh2, sc2, gt2 = jnp.split(mod[:, None, :], 6, axis=-1)
        h = rms_norm(x, g_pre_mix[l]) * (1.0 + sc1) + sh1
        hq, hf, hi, hgate, pv, aq, ak, av = jnp.split(h @ w_in[l], IN_SPLITS, axis=-1)
        o_a = hgrn2_mixer(hq, hf, hi, hgate, lb_all[l], hg_norm[l])
        o_b = pool_mixer(pv, pool_w[l], pool_scale[l])
        o_c = sink_swa_attention(aq, ak, av, attn_sinks[l])
        mix = jnp.concatenate([o_a, o_b, o_c], axis=-1) @ w_out[l]
        x = x + gt1 * rms_norm(mix, g_post_mix[l])
        h = rms_norm(x, g_pre_ffn[l]) * (1.0 + sc2) + sh2
        y = conv_ffn(h, w_up[l], conv_w[l], conv_b[l], w_down[l])
        x = x + gt2 * rms_norm(y, g_post_ffn[l])
    return x
```

```python
import functools

import jax
import jax.numpy as jnp
from jax import lax
from jax.experimental import pallas as pl
from jax.experimental.pallas import tpu as pltpu

F32 = jnp.float32
BF16 = jnp.bfloat16

D_MODEL = 1024
HG_HEADS = 4
HG_DK = 64
HG_DV = 64
HG_WIDTH = HG_HEADS * HG_DV
POOL_WINDOWS = (2, 4, 8, 16)
POOL_GDIM = 64
POOL_WIDTH = len(POOL_WINDOWS) * POOL_GDIM
POOL_TAIL = 16
ATT_HEADS = 8
ATT_KV_HEADS = 2
ATT_HDIM = 64
ATT_WIDTH = ATT_HEADS * ATT_HDIM
ATT_KV_WIDTH = ATT_KV_HEADS * ATT_HDIM
WINDOW = 128
ATT_BLOCK = WINDOW
D_FF = 2816
CONV_WIDTH = 3
EPS = 1e-6
D_IN = 4 * HG_WIDTH + POOL_WIDTH + ATT_WIDTH + 2 * ATT_KV_WIDTH
OFF_HQ = 0
OFF_HF = OFF_HQ + HG_WIDTH
OFF_HI = OFF_HF + HG_WIDTH
OFF_HG = OFF_HI + HG_WIDTH
OFF_PV = OFF_HG + HG_WIDTH
OFF_AQ = OFF_PV + POOL_WIDTH
OFF_AK = OFF_AQ + ATT_WIDTH
OFF_AV = OFF_AK + ATT_KV_WIDTH

HG_BLOCK = 16
MIX_TILE = 256
FFN_TILE = 512
FFN_CHUNK = 256
ADA_TILE = 1024
SUBLANES = 8
VMEM_LIMIT = 56 * 1024 * 1024


def _sigmoid(x):
    return 1.0 / (1.0 + jnp.exp(-x))


def _split3(x):
    hi = x.astype(BF16)
    r1 = x - hi.astype(F32)
    mid = r1.astype(BF16)
    lo = (r1 - mid.astype(F32)).astype(BF16)
    return hi, mid, lo


def _dot(a, b):
    return jnp.dot(a, b, preferred_element_type=F32)


def _dot_nt(a, b):
    return lax.dot_general(a, b, (((1,), (1,)), ((), ())), preferred_element_type=F32)


def _dot_tn(a, b):
    return lax.dot_general(a, b, (((0,), (0,)), ((), ())), preferred_element_type=F32)


def _ada_kernel(c_ref, w_ref, b_ref, lbl_ref, mod_ref, lbp_ref):
    c = c_ref[...]
    ca = c * _sigmoid(c)
    acc = None
    w_pieces = _split3(w_ref[0])[:2]
    for piece in _split3(ca):
        for wp in w_pieces:
            t = _dot(piece, wp)
            acc = t if acc is None else acc + t
    mod_ref[0] = acc + b_ref[0]

    logits = lbl_ref[...]
    depth = logits.shape[0]
    mx = jnp.max(logits, axis=0, keepdims=True)
    ex = jnp.exp(logits - mx)
    sm = ex / jnp.sum(ex, axis=0, keepdims=True)
    run = sm[0:1]
    first = run
    for l in range(depth):
        if l > 0:
            run = run + sm[l:l + 1]
        lb = run - first
        lbp_ref[l, 0:1, :] = jnp.log(lb)
        lbp_ref[l, 1:2, :] = jnp.log1p(-lb)
        lbp_ref[l, 2:3, :] = 1.0 - lb


def _ada_call(c, w_ada, b_ada, hg_lb_logits):
    depth, d, n = w_ada.shape
    bsz = c.shape[0]
    hk = hg_lb_logits.shape[1]
    grid = (depth, n // ADA_TILE)
    return pl.pallas_call(
        _ada_kernel,
        grid=grid,
        in_specs=[
            pl.BlockSpec((bsz, d), lambda l, j: (0, 0)),
            pl.BlockSpec((1, d, ADA_TILE), lambda l, j: (l, 0, j)),
            pl.BlockSpec((1, 1, ADA_TILE), lambda l, j: (l, 0, j)),
            pl.BlockSpec((depth, hk), lambda l, j: (0, 0)),
        ],
        out_specs=[
            pl.BlockSpec((1, bsz, ADA_TILE), lambda l, j: (l, 0, j)),
            pl.BlockSpec((depth, 3, hk), lambda l, j: (0, 0, 0)),
        ],
        out_shape=[
            jax.ShapeDtypeStruct((depth, bsz, n), F32),
            jax.ShapeDtypeStruct((depth, 3, hk), F32),
        ],
        compiler_params=pltpu.CompilerParams(
            dimension_semantics=("arbitrary", "arbitrary"), vmem_limit_bytes=VMEM_LIMIT),
        name="adaln_params",
    )(c, w_ada, b_ada.reshape(depth, 1, n), hg_lb_logits)


def _mix_kernel(x_ref, mod_ref, gpre_ref, gpost_ref, win_ref, lbp_ref, hgn_ref, poolw_ref,
                pools_ref, sink_ref, wout_ref, eblk_ref, cum_ref,
                o_ref,
                st_ref, kv_ref, ptail_ref, qf_s, b_s, kk_s, vi_s, qd_s, kd_s, dt_s, oa_s):
    s_idx = pl.program_id(1)
    tile = x_ref.shape[1]

    @pl.when(s_idx == 0)
    def _():
        st_ref[...] = jnp.zeros_like(st_ref)
        kv_ref[...] = jnp.zeros_like(kv_ref)
        ptail_ref[0:POOL_TAIL, :] = jnp.zeros((POOL_TAIL, POOL_WIDTH), F32)

    x = x_ref[0]
    sh1 = mod_ref[0, 0:1, :]
    sc1 = mod_ref[0, 1:2, :]
    gt1 = mod_ref[0, 2:3, :]
    ms = jnp.mean(x * x, axis=-1, keepdims=True)
    h = (x * lax.rsqrt(ms + EPS) * gpre_ref[...]) * (1.0 + sc1) + sh1
    proj = _dot(h.astype(BF16), win_ref[...])

    eblk = eblk_ref[...]

    q = proj[:, OFF_HQ:OFF_HQ + HG_WIDTH]
    z = proj[:, OFF_HF:OFF_HF + HG_WIDTH]
    vi = proj[:, OFF_HI:OFF_HI + HG_WIDTH]
    gate = proj[:, OFF_HG:OFF_HG + HG_WIDTH]
    qf = q * _sigmoid(q) * (HG_DK ** -0.5)
    log_sig = jnp.minimum(z, 0.0) - jnp.log1p(jnp.exp(-jnp.abs(z)))
    la = lbp_ref[0:1, :]
    lc = lbp_ref[1:2, :] + log_sig
    log_f = jnp.maximum(la, lc) + jnp.log1p(jnp.exp(-jnp.abs(la - lc)))
    kk = lbp_ref[2:3, :] * _sigmoid(-z)
    parts = jnp.concatenate(_split3(log_f), axis=1)
    cums = _dot(cum_ref[...], parts)
    w = HG_WIDTH
    b_loc = cums[:tile, 0:w] + cums[:tile, w:2 * w] + cums[:tile, 2 * w:3 * w]
    b_tot = cums[tile:, 0:w] + cums[tile:, w:2 * w] + cums[tile:, 2 * w:3 * w]
    qf_s[...] = qf
    b_s[...] = b_loc
    kk_s[...] = kk
    vi_s[...] = vi
    qd_s[...] = (qf * jnp.exp(b_loc)).astype(BF16)
    kd_s[...] = (kk * jnp.exp(b_tot - b_loc)).astype(BF16)
    dt_s[...] = jnp.exp(b_tot)

    rows16 = lax.broadcasted_iota(jnp.int32, (HG_BLOCK, HG_WIDTH), 0)
    emask = eblk > 0

    def hg_step(i, carry):
        r0 = pl.multiple_of(i * HG_BLOCK, HG_BLOCK)
        qb = qf_s[pl.ds(r0, HG_BLOCK), :]
        bb = b_s[pl.ds(r0, HG_BLOCK), :]
        pieces = []
        for j in range(HG_BLOCK):
            bj = b_s[pl.ds(r0 + j, 1), :]
            kj = kk_s[pl.ds(r0 + j, 1), :]
            d = jnp.where(rows16 >= j, (qb * kj) * jnp.exp(bb - bj), 0.0)
            pieces.append(d.astype(BF16))
        dall = jnp.concatenate(pieces, axis=0)
        sall = _dot(dall, eblk)
        acc = jnp.zeros((HG_BLOCK, HG_WIDTH), F32)
        for j in range(HG_BLOCK):
            vj = vi_s[pl.ds(r0 + j, 1), :]
            acc = acc + sall[j * HG_BLOCK:(j + 1) * HG_BLOCK, :] * vj
        st = st_ref[...]
        o_inter = _dot_nt(qd_s[pl.ds(r0, HG_BLOCK), :], st.astype(BF16))
        oa_s[pl.ds(r0, HG_BLOCK), :] = acc + o_inter
        vb = vi_s[pl.ds(r0, HG_BLOCK), :].astype(BF16)
        upd = _dot_tn(vb, kd_s[pl.ds(r0, HG_BLOCK), :])
        st_ref[...] = st * dt_s[pl.ds(r0, 1), :] + jnp.where(emask, upd, 0.0)
        return carry

    lax.fori_loop(0, tile // HG_BLOCK, hg_step, 0)

    oa = oa_s[...]
    sq_hi, sq_lo, _ = _split3(oa * oa)
    ms_h = (_dot(sq_hi, eblk) + _dot(sq_lo, eblk)) * (1.0 / HG_DV)
    o_a = oa * lax.rsqrt(ms_h + EPS) * hgn_ref[...] * (gate * _sigmoid(gate))

    pv = proj[:, OFF_PV:OFF_PV + POOL_WIDTH]
    ptail_ref[POOL_TAIL:POOL_TAIL + tile, :] = pv
    lane_p = lax.broadcasted_iota(jnp.int32, (1, POOL_WIDTH), 1)
    win_l = jnp.left_shift(2, lane_p // POOL_GDIM)
    acc_p = pv
    for dlt in range(1, max(POOL_WINDOWS)):
        shifted = ptail_ref[pl.ds(POOL_TAIL - dlt, tile), :]
        acc_p = acc_p + jnp.where(win_l > dlt, shifted, 0.0)
    t_glob = s_idx * tile + lax.broadcasted_iota(jnp.int32, (tile, POOL_WIDTH), 0)
    cnt = jnp.minimum(t_glob + 1, win_l).astype(F32)
    pooled = acc_p / cnt - pv
    ptail_ref[0:POOL_TAIL, :] = ptail_ref[pl.ds(tile, POOL_TAIL), :]
    o_b = _dot(pooled.astype(BF16), poolw_ref[...]) * pools_ref[...]

    lane_kv = lax.broadcasted_iota(jnp.int32, (ATT_BLOCK, ATT_KV_WIDTH), 1)
    low_half = lane_kv < ATT_HDIM
    qi = lax.broadcasted_iota(jnp.int32, (ATT_BLOCK, 2 * ATT_BLOCK), 0)
    kj = lax.broadcasted_iota(jnp.int32, (ATT_BLOCK, 2 * ATT_BLOCK), 1)
    in_window = (kj > qi) & (kj <= qi + ATT_BLOCK)
    group = ATT_HEADS // ATT_KV_HEADS
    prev = None
    o_c_blocks = []
    for a in range(tile // ATT_BLOCK):
        r = slice(a * ATT_BLOCK, (a + 1) * ATT_BLOCK)
        k_o = proj[r, OFF_AK:OFF_AK + ATT_KV_WIDTH]
        v_o = proj[r, OFF_AV:OFF_AV + ATT_KV_WIDTH]
        cur = (k_o.astype(BF16), pltpu.roll(k_o, ATT_HDIM, 1).astype(BF16),
               v_o.astype(BF16), pltpu.roll(v_o, ATT_HDIM, 1).astype(BF16))
        if prev is None:
            prev = tuple(kv_ref[n] for n in range(4))
            valid = in_window & ((kj >= ATT_BLOCK) | (s_idx > 0))
        else:
            valid = in_window
        kcat = [jnp.concatenate([prev[n], cur[n]], axis=0) for n in range(4)]
        pair_out = []
        for p in range(ATT_HEADS // 2):
            q_pair = proj[r, OFF_AQ + p * 2 * ATT_HDIM:OFF_AQ + (p + 1) * 2 * ATT_HDIM] * (ATT_HDIM ** -0.5)
            kvh = (2 * p) // group
            halves = []
            for half in range(2):
                head = 2 * p + half
                use_swapped = (half != kvh)
                keep = low_half if half == 0 else jnp.logical_not(low_half)
                qh = jnp.where(keep, q_pair, 0.0).astype(BF16)
                kmat = kcat[1] if use_swapped else kcat[0]
                vmat = kcat[3] if use_swapped else kcat[2]
                s = _dot_nt(qh, kmat)
                s = jnp.where(valid, s, -jnp.inf)
                sink = sink_ref[head]
                m = jnp.maximum(jnp.max(s, axis=-1, keepdims=True), sink)
                pexp = jnp.exp(s - m)
                denom = jnp.sum(pexp, axis=-1, keepdims=True) + jnp.exp(sink - m)
                halves.append(_dot(pexp.astype(BF16), vmat) / denom)
            pair_out.append(jnp.where(low_half, halves[0], halves[1]))
        o_c_blocks.append(jnp.concatenate(pair_out, axis=1))
        prev = cur
    for n in range(4):
        kv_ref[n] = prev[n]
    o_c = jnp.concatenate(o_c_blocks, axis=0)

    cat = jnp.concatenate([o_a.astype(BF16), o_b.astype(BF16), o_c.astype(BF16)], axis=1)
    mix = _dot(cat, wout_ref[...])
    msm = jnp.mean(mix * mix, axis=-1, keepdims=True)
    o_ref[0] = x + gt1 * (mix * lax.rsqrt(msm + EPS) * gpost_ref[...])


def _const_spec(shape):
    nd = len(shape)
    return pl.BlockSpec(shape, lambda b, s, _nd=nd: (0,) * _nd, pipeline_mode=pl.Buffered(1))


def _mix_call(x, mod, g_pre, g_post, w_in, lbp, hg_norm_t, pool_bd, pool_scale, sinks, w_out, eblk, cum):
    bsz, seq, d = x.shape
    tile = min(MIX_TILE, seq)
    grid = (bsz, seq // tile)
    row = lambda v: v.reshape(1, -1)
    in_specs = [
        pl.BlockSpec((1, tile, d), lambda b, s: (b, s, 0)),
        pl.BlockSpec((1, 6, d), lambda b, s: (b, 0, 0)),
        _const_spec((1, d)), _const_spec((1, d)),
        _const_spec(w_in.shape),
        _const_spec(lbp.shape),
        _const_spec((1, HG_WIDTH)),
        _const_spec(pool_bd.shape),
        _const_spec((1, POOL_WIDTH)),
        pl.BlockSpec(memory_space=pltpu.SMEM),
        _const_spec(w_out.shape),
        _const_spec(eblk.shape),
        _const_spec(cum.shape),
    ]
    scratch = [
        pltpu.VMEM((HG_WIDTH, HG_WIDTH), F32),
        pltpu.VMEM((4, ATT_BLOCK, ATT_KV_WIDTH), BF16),
        pltpu.VMEM((POOL_TAIL + tile, POOL_WIDTH), F32),
        pltpu.VMEM((tile, HG_WIDTH), F32),
        pltpu.VMEM((tile, HG_WIDTH), F32),
        pltpu.VMEM((tile, HG_WIDTH), F32),
        pltpu.VMEM((tile, HG_WIDTH), F32),
        pltpu.VMEM((tile, HG_WIDTH), BF16),
        pltpu.VMEM((tile, HG_WIDTH), BF16),
        pltpu.VMEM((tile, HG_WIDTH), F32),
        pltpu.VMEM((tile, HG_WIDTH), F32),
    ]
    return pl.pallas_call(
        _mix_kernel,
        grid=grid,
        in_specs=in_specs,
        out_specs=pl.BlockSpec((1, tile, d), lambda b, s: (b, s, 0)),
        out_shape=jax.ShapeDtypeStruct(x.shape, F32),
        scratch_shapes=scratch,
        compiler_params=pltpu.CompilerParams(
            dimension_semantics=("arbitrary", "arbitrary"), vmem_limit_bytes=VMEM_LIMIT),
        name="token_mixer",
    )(x, mod, row(g_pre), row(g_post), w_in, lbp, row(hg_norm_t), pool_bd, row(pool_scale), sinks,
      w_out, eblk, cum)


def _gelu_tanh(x):
    c = 0.7978845608028654
    return 0.5 * x * (1.0 + jnp.tanh(c * (x + 0.044715 * (x * x * x))))


def _ffn_kernel(x_ref, mod_ref, gpre_ref, gpost_ref, wup_ref, cw_ref, cb_ref, wdown_ref,
                o_ref,
                tail_ref, ubuf_ref, act_ref):
    s_idx = pl.program_id(1)
    tile = x_ref.shape[1]

    @pl.when(s_idx == 0)
    def _():
        tail_ref[...] = jnp.zeros_like(tail_ref)

    x = x_ref[0]
    sh2 = mod_ref[0, 3:4, :]
    sc2 = mod_ref[0, 4:5, :]
    gt2 = mod_ref[0, 5:6, :]
    ms = jnp.mean(x * x, axis=-1, keepdims=True)
    h = (x * lax.rsqrt(ms + EPS) * gpre_ref[...]) * (1.0 + sc2) + sh2
    hb = h.astype(BF16)

    def conv_cols(off):
        u = _dot(hb, wup_ref[:, off:off + FFN_CHUNK])
        ubuf_ref[0:SUBLANES, :] = tail_ref[:, off:off + FFN_CHUNK]
        ubuf_ref[SUBLANES:SUBLANES + tile, :] = u
        tail_ref[:, off:off + FFN_CHUNK] = ubuf_ref[pl.ds(tile, SUBLANES), :]
        w0 = cw_ref[0:1, off:off + FFN_CHUNK]
        w1 = cw_ref[1:2, off:off + FFN_CHUNK]
        w2 = cw_ref[2:3, off:off + FFN_CHUNK]
        return (w0 * ubuf_ref[pl.ds(SUBLANES - 2, tile), :] + w1 * ubuf_ref[pl.ds(SUBLANES - 1, tile), :]
                + w2 * u + cb_ref[0:1, off:off + FFN_CHUNK])

    for c in range(D_FF // FFN_CHUNK):
        g = conv_cols(c * FFN_CHUNK)
        v = conv_cols(D_FF + c * FFN_CHUNK)
        act_ref[:, c * FFN_CHUNK:(c + 1) * FFN_CHUNK] = (_gelu_tanh(g) * v).astype(BF16)

    y = _dot(act_ref[...], wdown_ref[...])
    msy = jnp.mean(y * y, axis=-1, keepdims=True)
    o_ref[0] = x + gt2 * (y * lax.rsqrt(msy + EPS) * gpost_ref[...])


def _ffn_call(x, mod, g_pre, g_post, w_up, conv_w, conv_b, w_down):
    bsz, seq, d = x.shape
    tile = min(FFN_TILE, seq)
    grid = (bsz, seq // tile)
    row = lambda v: v.reshape(1, -1)
    in_specs = [
        pl.BlockSpec((1, tile, d), lambda b, s: (b, s, 0)),
        pl.BlockSpec((1, 6, d), lambda b, s: (b, 0, 0)),
        _const_spec((1, d)), _const_spec((1, d)),
        _const_spec(w_up.shape),
        _const_spec(conv_w.shape),
        _const_spec((1, 2 * D_FF)),
        _const_spec(w_down.shape),
    ]
    scratch = [
        pltpu.VMEM((SUBLANES, 2 * D_FF), F32),
        pltpu.VMEM((SUBLANES + tile, FFN_CHUNK), F32),
        pltpu.VMEM((tile, D_FF), BF16),
    ]
    return pl.pallas_call(
        _ffn_kernel,
        grid=grid,
        in_specs=in_specs,
        out_specs=pl.BlockSpec((1, tile, d), lambda b, s: (b, s, 0)),
        out_shape=jax.ShapeDtypeStruct(x.shape, F32),
        scratch_shapes=scratch,
        compiler_params=pltpu.CompilerParams(
            dimension_semantics=("arbitrary", "arbitrary"), vmem_limit_bytes=VMEM_LIMIT),
        name="conv_ffn",
    )(x, mod, row(g_pre), row(g_post), w_up, conv_w, row(conv_b), w_down)


def _block_diag(blocks):
    g, c, _ = blocks.shape
    eye = jnp.eye(g, dtype=blocks.dtype)
    return (eye[:, None, :, None] * blocks[:, :, None, :]).reshape(g * c, g * c)


def _cum_matrix(tile):
    r = jnp.arange(tile)
    same = (r[:, None] // HG_BLOCK) == (r[None, :] // HG_BLOCK)
    tri = same & (r[None, :] <= r[:, None])
    return jnp.concatenate([tri, same], axis=0).astype(BF16)


def kernel(x, c, w_ada, b_ada, g_pre_mix, g_post_mix, w_in, hg_lb_logits, hg_norm, pool_w, pool_scale,
           attn_sinks, w_out, g_pre_ffn, g_post_ffn, w_up, conv_w, conv_b, w_down):
    depth = w_ada.shape[0]
    bsz, seq, d = x.shape
    mod_all, lbp_all = _ada_call(c, w_ada, b_ada, hg_lb_logits)
    mod_all = mod_all.reshape(depth, bsz, 6, d)
    eblk = _block_diag(jnp.ones((HG_HEADS, HG_DK, HG_DV), BF16))
    cum = _cum_matrix(min(MIX_TILE, seq))
    for l in range(depth):
        x = _mix_call(
            x, mod_all[l], g_pre_mix[l], g_post_mix[l], w_in[l].astype(BF16), lbp_all[l],
            jnp.tile(hg_norm[l], HG_HEADS), _block_diag(pool_w[l]).astype(BF16), pool_scale[l],
            attn_sinks[l], w_out[l].astype(BF16), eblk, cum)
        x = _ffn_call(
            x, mod_all[l], g_pre_ffn[l], g_post_ffn[l], w_up[l].astype(BF16), conv_w[l], conv_b[l],
            w_down[l].astype(BF16))
    return x
```

```python
import jax
import jax.numpy as jnp
from jax import lax
from jax.experimental import pallas as pl
from jax.experimental.pallas import tpu as pltpu

F32 = jnp.float32
BF16 = jnp.bfloat16

D_MODEL = 1024
HG_HEADS = 4
HG_DK = 64
HG_DV = 64
HG_WIDTH = HG_HEADS * HG_DV
POOL_WINDOWS = (2, 4, 8, 16)
POOL_GDIM = 64
POOL_WIDTH = len(POOL_WINDOWS) * POOL_GDIM
POOL_TAIL = 16
ATT_HEADS = 8
ATT_KV_HEADS = 2
ATT_HDIM = 64
ATT_WIDTH = ATT_HEADS * ATT_HDIM
ATT_KV_WIDTH = ATT_KV_HEADS * ATT_HDIM
WINDOW = 128
ATT_BLOCK = WINDOW
D_FF = 2816
CONV_WIDTH = 3
EPS = 1e-6
LOG2E = 1.4426950408889634
D_IN = 4 * HG_WIDTH + POOL_WIDTH + ATT_WIDTH + 2 * ATT_KV_WIDTH
OFF_HQ = 0
OFF_HF = OFF_HQ + HG_WIDTH
OFF_HI = OFF_HF + HG_WIDTH
OFF_HG = OFF_HI + HG_WIDTH
OFF_PV = OFF_HG + HG_WIDTH
OFF_AQ = OFF_PV + POOL_WIDTH
OFF_AK = OFF_AQ + ATT_WIDTH
OFF_AV = OFF_AK + ATT_KV_WIDTH

HG_CHUNK = 64
HG_BLOCK = 16
HG_SAFE_EXP = 80.0
MIX_TILE = 256
FFN_TILE = 512
FFN_CHUNK = 256
ADA_TILE = 1024
SUBLANES = 8
VMEM_LIMIT = 56 * 1024 * 1024


def _sigmoid(x):
    return 1.0 / (1.0 + jnp.exp(-x))


def _split3(x):
    hi = x.astype(BF16)
    r1 = x - hi.astype(F32)
    mid = r1.astype(BF16)
    lo = (r1 - mid.astype(F32)).astype(BF16)
    return hi, mid, lo


def _dot(a, b):
    return jnp.dot(a, b, preferred_element_type=F32)


def _dot_nt(a, b):
    return lax.dot_general(a, b, (((1,), (1,)), ((), ())), preferred_element_type=F32)


def _dot_tn(a, b):
    return lax.dot_general(a, b, (((0,), (0,)), ((), ())), preferred_element_type=F32)


def _ada_kernel(c_ref, w_ref, b_ref, lbl_ref, mod_ref, lbp_ref):
    c = c_ref[...]
    ca = c * _sigmoid(c)
    acc = None
    w_pieces = _split3(w_ref[0])[:2]
    for piece in _split3(ca):
        for wp in w_pieces:
            t = _dot(piece, wp)
            acc = t if acc is None else acc + t
    mod_ref[0] = acc + b_ref[0]

    logits = lbl_ref[...]
    depth = logits.shape[0]
    mx = jnp.max(logits, axis=0, keepdims=True)
    ex = jnp.exp(logits - mx)
    sm = ex / jnp.sum(ex, axis=0, keepdims=True)
    run = sm[0:1]
    first = run
    for l in range(depth):
        if l > 0:
            run = run + sm[l:l + 1]
        lb = run - first
        lbp_ref[l, 0:1, :] = jnp.log(lb)
        lbp_ref[l, 1:2, :] = jnp.log1p(-lb)
        lbp_ref[l, 2:3, :] = 1.0 - lb


def _ada_call(c, w_ada, b_ada, hg_lb_logits):
    depth, d, n = w_ada.shape
    bsz = c.shape[0]
    hk = hg_lb_logits.shape[1]
    grid = (depth, n // ADA_TILE)
    return pl.pallas_call(
        _ada_kernel,
        grid=grid,
        in_specs=[
            pl.BlockSpec((bsz, d), lambda l, j: (0, 0)),
            pl.BlockSpec((1, d, ADA_TILE), lambda l, j: (l, 0, j)),
            pl.BlockSpec((1, 1, ADA_TILE), lambda l, j: (l, 0, j)),
            pl.BlockSpec((depth, hk), lambda l, j: (0, 0)),
        ],
        out_specs=[
            pl.BlockSpec((1, bsz, ADA_TILE), lambda l, j: (l, 0, j)),
            pl.BlockSpec((depth, 3, hk), lambda l, j: (0, 0, 0)),
        ],
        out_shape=[
            jax.ShapeDtypeStruct((depth, bsz, n), F32),
            jax.ShapeDtypeStruct((depth, 3, hk), F32),
        ],
        compiler_params=pltpu.CompilerParams(
            dimension_semantics=("arbitrary", "arbitrary"), vmem_limit_bytes=VMEM_LIMIT),
        name="adaln_params",
    )(c, w_ada, b_ada.reshape(depth, 1, n), hg_lb_logits)


def _kv_variants(t, low_half):
    sw = pltpu.roll(t, ATT_HDIM, 1)
    return (jnp.where(low_half, t, 0.0).astype(BF16), jnp.where(low_half, 0.0, sw).astype(BF16),
            jnp.where(low_half, sw, 0.0).astype(BF16), jnp.where(low_half, 0.0, t).astype(BF16))


def _mix_kernel(x_ref, mod_ref, gpre_ref, gpost_ref, win_ref, lbp_ref, hgn_ref, poolw_ref,
                pools_ref, sink_ref, wout_ref, eblk_ref, cum_ref,
                o_ref,
                st_ref, kv_ref, ptail_ref, qf_s, b_s, kk_s, vi_s, oa_s):
    s_idx = pl.program_id(1)
    tile = x_ref.shape[1]

    @pl.when(s_idx == 0)
    def _():
        st_ref[...] = jnp.zeros_like(st_ref)
        kv_ref[...] = jnp.zeros_like(kv_ref)
        ptail_ref[0:POOL_TAIL, :] = jnp.zeros((POOL_TAIL, POOL_WIDTH), F32)

    x = x_ref[0]
    sh1 = mod_ref[0, 0:1, :]
    sc1 = mod_ref[0, 1:2, :]
    gt1 = mod_ref[0, 2:3, :]
    ms = jnp.mean(x * x, axis=-1, keepdims=True)
    h = (x * lax.rsqrt(ms + EPS) * gpre_ref[...]) * (1.0 + sc1) + sh1
    proj = _dot(h.astype(BF16), win_ref[0])

    eblk = eblk_ref[...]
    emask = eblk > 0

    q = proj[:, OFF_HQ:OFF_HQ + HG_WIDTH]
    z = proj[:, OFF_HF:OFF_HF + HG_WIDTH]
    vi = proj[:, OFF_HI:OFF_HI + HG_WIDTH]
    gate = proj[:, OFF_HG:OFF_HG + HG_WIDTH]
    qf = q * _sigmoid(q) * (HG_DK ** -0.5)
    log_sig = jnp.minimum(z, 0.0) - jnp.log1p(jnp.exp(-jnp.abs(z)))
    la = lbp_ref[0, 0:1, :]
    lc = lbp_ref[0, 1:2, :] + log_sig
    log_f = jnp.maximum(la, lc) + jnp.log1p(jnp.exp(-jnp.abs(la - lc)))
    kk = lbp_ref[0, 2:3, :] * _sigmoid(-z)
    parts = jnp.concatenate(_split3(log_f), axis=1)
    cums = _dot(cum_ref[...], parts)
    w = HG_WIDTH
    b_loc = cums[:, 0:w] + cums[:, w:2 * w] + cums[:, 2 * w:3 * w]

    n_chunks = tile // HG_CHUNK
    mid = HG_CHUNK // 2 - 1
    spread = None
    for c in range(n_chunks):
        bl = b_loc[c * HG_CHUNK:(c + 1) * HG_CHUNK]
        sp = jnp.max(jnp.abs(bl - bl[mid:mid + 1]))
        spread = sp if spread is None else jnp.maximum(spread, sp)
    factorisable = spread <= HG_SAFE_EXP

    lane_head = lax.broadcasted_iota(jnp.int32, (HG_CHUNK, HG_WIDTH), 1) // HG_DK
    ri = lax.broadcasted_iota(jnp.int32, (HG_HEADS * HG_CHUNK, HG_CHUNK), 0) % HG_CHUNK
    ci = lax.broadcasted_iota(jnp.int32, (HG_HEADS * HG_CHUNK, HG_CHUNK), 1)
    causal = ri >= ci
    st = st_ref[...]
    oa_chunks = []
    for c in range(n_chunks):
        r = slice(c * HG_CHUNK, (c + 1) * HG_CHUNK)
        bl = b_loc[r]
        b_mid = bl[mid:mid + 1]
        b_end = bl[HG_CHUNK - 1:HG_CHUNK]
        qb = qf[r]
        kb = kk[r]
        vb = vi[r].astype(BF16)
        qm = qb * jnp.exp(bl - b_mid)
        kn = (kb * jnp.exp(b_mid - bl)).astype(BF16)
        qs = (qb * jnp.exp(bl)).astype(BF16)
        ke = (kb * jnp.exp(b_end - bl)).astype(BF16)
        q4 = jnp.concatenate(
            [jnp.where(lane_head == hd, qm, 0.0).astype(BF16) for hd in range(HG_HEADS)], axis=0)
        s4 = _dot_nt(q4, kn)
        p4 = jnp.where(causal, s4, 0.0).astype(BF16)
        o4 = _dot(p4, vb)
        o_intra = jnp.where(lane_head == 0, o4[0:HG_CHUNK], 0.0)
        for hd in range(1, HG_HEADS):
            o_intra = o_intra + jnp.where(lane_head == hd, o4[hd * HG_CHUNK:(hd + 1) * HG_CHUNK], 0.0)
        oa_chunks.append(o_intra + _dot_nt(qs, st.astype(BF16)))
        st = st * jnp.exp(b_end) + jnp.where(emask, _dot_tn(vb, ke), 0.0)
    oa_fast = jnp.concatenate(oa_chunks, axis=0)
    st_fast = st

    pv = proj[:, OFF_PV:OFF_PV + POOL_WIDTH]
    ptail_ref[POOL_TAIL:POOL_TAIL + tile, :] = pv
    lane_p = lax.broadcasted_iota(jnp.int32, (1, POOL_WIDTH), 1)
    win_l = jnp.left_shift(2, lane_p // POOL_GDIM)
    acc_p = pv
    for dlt in range(1, max(POOL_WINDOWS)):
        shifted = ptail_ref[pl.ds(POOL_TAIL - dlt, tile), :]
        acc_p = acc_p + jnp.where(win_l > dlt, shifted, 0.0)
    t_glob = s_idx * tile + lax.broadcasted_iota(jnp.int32, (tile, POOL_WIDTH), 0)
    cnt = jnp.minimum(t_glob + 1, win_l).astype(F32)
    pooled = acc_p / cnt - pv
    ptail_ref[0:POOL_TAIL, :] = ptail_ref[pl.ds(tile, POOL_TAIL), :]
    o_b = _dot(pooled.astype(BF16), poolw_ref[...]) * pools_ref[...]

    group = ATT_HEADS // ATT_KV_HEADS
    pair_rows = (group // 2) * ATT_BLOCK
    low_half = lax.broadcasted_iota(jnp.int32, (ATT_BLOCK, ATT_KV_WIDTH), 1) < ATT_HDIM
    qi = lax.broadcasted_iota(jnp.int32, (pair_rows, ATT_BLOCK), 0) % ATT_BLOCK
    kj = lax.broadcasted_iota(jnp.int32, (pair_rows, ATT_BLOCK), 1)
    upper = kj > qi
    first_pair = lax.broadcasted_iota(jnp.int32, (pair_rows, 1), 0) < ATT_BLOCK
    q_scale = (ATT_HDIM ** -0.5) * LOG2E
    prev = None
    o_c_blocks = []
    for a in range(tile // ATT_BLOCK):
        r = slice(a * ATT_BLOCK, (a + 1) * ATT_BLOCK)
        cur_k = _kv_variants(proj[r, OFF_AK:OFF_AK + ATT_KV_WIDTH], low_half)
        cur_v = _kv_variants(proj[r, OFF_AV:OFF_AV + ATT_KV_WIDTH], low_half)
        if prev is None:
            prev_k = tuple(kv_ref[n] for n in range(4))
            prev_v = tuple(kv_ref[4 + n] for n in range(4))
            no_prev = jnp.where(upper & (s_idx == 0), -jnp.inf, 0.0)
        else:
            prev_k, prev_v = prev
            no_prev = None
        pair_out = []
        for g in range(ATT_KV_HEADS):
            c0 = OFF_AQ + g * group * ATT_HDIM
            q2 = jnp.concatenate(
                [(proj[r, c0 + p * 2 * ATT_HDIM:c0 + (p + 1) * 2 * ATT_HDIM] * q_scale).astype(BF16)
                 for p in range(group // 2)], axis=0)
            kcat = jnp.concatenate([prev_k[2 * g], cur_k[2 * g], prev_k[2 * g + 1], cur_k[2 * g + 1]], axis=0)
            s_all = _dot_nt(q2, kcat)
            o_g = None
            for half in range(2):
                s_prev = s_all[:, (2 * half) * ATT_BLOCK:(2 * half + 1) * ATT_BLOCK]
                s_cur = s_all[:, (2 * half + 1) * ATT_BLOCK:(2 * half + 2) * ATT_BLOCK]
                if no_prev is not None:
                    s_prev = s_prev + no_prev
                s = jnp.where(upper, s_prev, s_cur)
                head0 = g * group + half
                sink = jnp.where(first_pair, sink_ref[head0], sink_ref[head0 + 2]) * LOG2E
                m = jnp.maximum(jnp.max(s, axis=-1, keepdims=True), sink)
                pexp = jnp.exp2(s - m)
                denom = jnp.sum(pexp, axis=-1, keepdims=True) + jnp.exp2(sink - m)
                pcat = jnp.concatenate([jnp.where(upper, pexp, 0.0).astype(BF16),
                                        jnp.where(upper, 0.0, pexp).astype(BF16)], axis=1)
                vcat = jnp.concatenate([prev_v[2 * g + half], cur_v[2 * g + half]], axis=0)
                o_h = _dot(pcat, vcat) * (1.0 / denom)
                o_g = o_h if o_g is None else o_g + o_h
            for p in range(group // 2):
                pair_out.append(o_g[p * ATT_BLOCK:(p + 1) * ATT_BLOCK])
        o_c_blocks.append(jnp.concatenate(pair_out, axis=1))
        prev = (cur_k, cur_v)
    for n in range(4):
        kv_ref[n] = prev[0][n]
        kv_ref[4 + n] = prev[1][n]
    o_c = jnp.concatenate(o_c_blocks, axis=0)

    oa_s[...] = oa_fast

    @pl.when(factorisable)
    def _():
        st_ref[...] = st_fast

    @pl.when(jnp.logical_not(factorisable))
    def _():
        qf_s[...] = qf
        b_s[...] = b_loc
        kk_s[...] = kk
        vi_s[...] = vi
        rows16 = lax.broadcasted_iota(jnp.int32, (HG_BLOCK, HG_WIDTH), 0)
        per_chunk = HG_CHUNK // HG_BLOCK

        def hg_step(i, carry):
            r0 = pl.multiple_of(i * HG_BLOCK, HG_BLOCK)
            qb = qf_s[pl.ds(r0, HG_BLOCK), :]
            bb = b_s[pl.ds(r0, HG_BLOCK), :]
            kb = kk_s[pl.ds(r0, HG_BLOCK), :]
            b_prev = jnp.where(i % per_chunk == 0, 0.0, b_s[pl.ds(jnp.maximum(r0 - 1, 0), 1), :])
            b_end = b_s[pl.ds(r0 + HG_BLOCK - 1, 1), :]
            pieces = []
            for j in range(HG_BLOCK):
                bj = b_s[pl.ds(r0 + j, 1), :]
                kj_row = kk_s[pl.ds(r0 + j, 1), :]
                d = jnp.where(rows16 >= j, (qb * kj_row) * jnp.exp(bb - bj), 0.0)
                pieces.append(d.astype(BF16))
            dall = jnp.concatenate(pieces, axis=0)
            sall = _dot(dall, eblk)
            acc = jnp.zeros((HG_BLOCK, HG_WIDTH), F32)
            for j in range(HG_BLOCK):
                vj = vi_s[pl.ds(r0 + j, 1), :]
                acc = acc + sall[j * HG_BLOCK:(j + 1) * HG_BLOCK, :] * vj
            st_old = st_ref[...]
            qs_b = (qb * jnp.exp(bb - b_prev)).astype(BF16)
            ke_b = (kb * jnp.exp(b_end - bb)).astype(BF16)
            oa_s[pl.ds(r0, HG_BLOCK), :] = acc + _dot_nt(qs_b, st_old.astype(BF16))
            vb_b = vi_s[pl.ds(r0, HG_BLOCK), :].astype(BF16)
            st_ref[...] = st_old * jnp.exp(b_end - b_prev) + jnp.where(emask, _dot_tn(vb_b, ke_b), 0.0)
            return carry

        lax.fori_loop(0, tile // HG_BLOCK, hg_step, 0)

    oa = oa_s[...]
    sq_hi, sq_lo, _ = _split3(oa * oa)
    ms_h = (_dot(sq_hi, eblk) + _dot(sq_lo, eblk)) * (1.0 / HG_DV)
    o_a = oa * lax.rsqrt(ms_h + EPS) * hgn_ref[...] * (gate * _sigmoid(gate))

    cat = jnp.concatenate([o_a.astype(BF16), o_b.astype(BF16), o_c.astype(BF16)], axis=1)
    mix = _dot(cat, wout_ref[0])
    msm = jnp.mean(mix * mix, axis=-1, keepdims=True)
    o_ref[0] = x + gt1 * (mix * lax.rsqrt(msm + EPS) * gpost_ref[...])


def _const_spec(shape):
    nd = len(shape)
    return pl.BlockSpec(shape, lambda b, s, _nd=nd: (0,) * _nd, pipeline_mode=pl.Buffered(1))


def _layer_spec(shape, layer):
    nd = len(shape)
    return pl.BlockSpec((1,) + tuple(shape[1:]), lambda b, s, _nd=nd, _l=layer: (_l,) + (0,) * (_nd - 1),
                        pipeline_mode=pl.Buffered(1))


def _mix_call(layer, x, mod, g_pre, g_post, w_in, lbp, hg_norm_t, pool_bd, pool_scale, sinks, w_out, eblk, cum):
    bsz, seq, d = x.shape
    tile = min(MIX_TILE, seq)
    grid = (bsz, seq // tile)
    row = lambda v: v.reshape(1, -1)
    in_specs = [
        pl.BlockSpec((1, tile, d), lambda b, s: (b, s, 0)),
        pl.BlockSpec((1, 6, d), lambda b, s: (b, 0, 0)),
        _const_spec((1, d)), _const_spec((1, d)),
        _layer_spec(w_in.shape, layer),
        _layer_spec(lbp.shape, layer),
        _const_spec((1, HG_WIDTH)),
        _const_spec(pool_bd.shape),
        _const_spec((1, POOL_WIDTH)),
        pl.BlockSpec(memory_space=pltpu.SMEM),
        _layer_spec(w_out.shape, layer),
        _const_spec(eblk.shape),
        _const_spec(cum.shape),
    ]
    scratch = [
        pltpu.VMEM((HG_WIDTH, HG_WIDTH), F32),
        pltpu.VMEM((8, ATT_BLOCK, ATT_KV_WIDTH), BF16),
        pltpu.VMEM((POOL_TAIL + tile, POOL_WIDTH), F32),
        pltpu.VMEM((tile, HG_WIDTH), F32),
        pltpu.VMEM((tile, HG_WIDTH), F32),
        pltpu.VMEM((tile, HG_WIDTH), F32),
        pltpu.VMEM((tile, HG_WIDTH), F32),
        pltpu.VMEM((tile, HG_WIDTH), F32),
    ]
    return pl.pallas_call(
        _mix_kernel,
        grid=grid,
        in_specs=in_specs,
        out_specs=pl.BlockSpec((1, tile, d), lambda b, s: (b, s, 0)),
        out_shape=jax.ShapeDtypeStruct(x.shape, F32),
        scratch_shapes=scratch,
        compiler_params=pltpu.CompilerParams(
            dimension_semantics=("arbitrary", "arbitrary"), vmem_limit_bytes=VMEM_LIMIT),
        name="token_mixer",
    )(x, mod, row(g_pre), row(g_post), w_in, lbp, row(hg_norm_t), pool_bd, row(pool_scale), sinks,
      w_out, eblk, cum)


def _gelu_tanh(x):
    c = 0.7978845608028654
    return 0.5 * x * (1.0 + jnp.tanh(c * (x + 0.044715 * (x * x * x))))


def _ffn_kernel(x_ref, mod_ref, gpre_ref, gpost_ref, wup_ref, cw_ref, cb_ref, wdown_ref,
                o_ref,
                tail_ref, ubuf_ref, act_ref):
    s_idx = pl.program_id(1)
    tile = x_ref.shape[1]

    @pl.when(s_idx == 0)
    def _():
        tail_ref[...] = jnp.zeros_like(tail_ref)

    x = x_ref[0]
    sh2 = mod_ref[0, 3:4, :]
    sc2 = mod_ref[0, 4:5, :]
    gt2 = mod_ref[0, 5:6, :]
    ms = jnp.mean(x * x, axis=-1, keepdims=True)
    h = (x * lax.rsqrt(ms + EPS) * gpre_ref[...]) * (1.0 + sc2) + sh2
    hb = h.astype(BF16)

    def conv_cols(off):
        u = _dot(hb, wup_ref[0, :, off:off + FFN_CHUNK])
        ubuf_ref[0:SUBLANES, :] = tail_ref[:, off:off + FFN_CHUNK]
        ubuf_ref[SUBLANES:SUBLANES + tile, :] = u
        tail_ref[:, off:off + FFN_CHUNK] = ubuf_ref[pl.ds(tile, SUBLANES), :]
        w0 = cw_ref[0:1, off:off + FFN_CHUNK]
        w1 = cw_ref[1:2, off:off + FFN_CHUNK]
        w2 = cw_ref[2:3, off:off + FFN_CHUNK]
        return (w0 * ubuf_ref[pl.ds(SUBLANES - 2, tile), :] + w1 * ubuf_ref[pl.ds(SUBLANES - 1, tile), :]
                + w2 * u + cb_ref[0:1, off:off + FFN_CHUNK])

    for c in range(D_FF // FFN_CHUNK):
        g = conv_cols(c * FFN_CHUNK)
        v = conv_cols(D_FF + c * FFN_CHUNK)
        act_ref[:, c * FFN_CHUNK:(c + 1) * FFN_CHUNK] = (_gelu_tanh(g) * v).astype(BF16)

    y = _dot(act_ref[...], wdown_ref[0])
    msy = jnp.mean(y * y, axis=-1, keepdims=True)
    o_ref[0] = x + gt2 * (y * lax.rsqrt(msy + EPS) * gpost_ref[...])


def _ffn_call(layer, x, mod, g_pre, g_post, w_up, conv_w, conv_b, w_down):
    bsz, seq, d = x.shape
    tile = min(FFN_TILE, seq)
    grid = (bsz, seq // tile)
    row = lambda v: v.reshape(1, -1)
    in_specs = [
        pl.BlockSpec((1, tile, d), lambda b, s: (b, s, 0)),
        pl.BlockSpec((1, 6, d), lambda b, s: (b, 0, 0)),
        _const_spec((1, d)), _const_spec((1, d)),
        _layer_spec(w_up.shape, layer),
        _const_spec(conv_w.shape),
        _const_spec((1, 2 * D_FF)),
        _layer_spec(w_down.shape, layer),
    ]
    scratch = [
        pltpu.VMEM((SUBLANES, 2 * D_FF), F32),
        pltpu.VMEM((SUBLANES + tile, FFN_CHUNK), F32),
        pltpu.VMEM((tile, D_FF), BF16),
    ]
    return pl.pallas_call(
        _ffn_kernel,
        grid=grid,
        in_specs=in_specs,
        out_specs=pl.BlockSpec((1, tile, d), lambda b, s: (b, s, 0)),
        out_shape=jax.ShapeDtypeStruct(x.shape, F32),
        scratch_shapes=scratch,
        compiler_params=pltpu.CompilerParams(
            dimension_semantics=("arbitrary", "arbitrary"), vmem_limit_bytes=VMEM_LIMIT),
        name="conv_ffn",
    )(x, mod, row(g_pre), row(g_post), w_up, conv_w, row(conv_b), w_down)


def _block_diag(blocks):
    g, c, _ = blocks.shape
    eye = jnp.eye(g, dtype=blocks.dtype)
    return (eye[:, None, :, None] * blocks[:, :, None, :]).reshape(g * c, g * c)


def _cum_matrix(tile):
    r = jnp.arange(tile)
    same = (r[:, None] // HG_CHUNK) == (r[None, :] // HG_CHUNK)
    return (same & (r[None, :] <= r[:, None])).astype(BF16)


def kernel(x, c, w_ada, b_ada, g_pre_mix, g_post_mix, w_in, hg_lb_logits, hg_norm, pool_w, pool_scale,
           attn_sinks, w_out, g_pre_ffn, g_post_ffn, w_up, conv_w, conv_b, w_down):
    depth = w_ada.shape[0]
    bsz, seq, d = x.shape
    mod_all, lbp_all = _ada_call(c, w_ada, b_ada, hg_lb_logits)
    mod_all = mod_all.reshape(depth, bsz, 6, d)
    eblk = _block_diag(jnp.ones((HG_HEADS, HG_DK, HG_DV), BF16))
    cum = _cum_matrix(min(MIX_TILE, seq))
    w_in_b, w_out_b, w_up_b, w_down_b = (t.astype(BF16) for t in (w_in, w_out, w_up, w_down))
    for l in range(depth):
        x = _mix_call(
            l, x, mod_all[l], g_pre_mix[l], g_post_mix[l], w_in_b, lbp_all,
            jnp.tile(hg_norm[l], HG_HEADS), _block_diag(pool_w[l]).astype(BF16), pool_scale[l],
            attn_sinks[l], w_out_b, eblk, cum)
        x = _ffn_call(
            l, x, mod_all[l], g_pre_ffn[l], g_post_ffn[l], w_up_b, conv_w[l], conv_b[l], w_down_b)
    return x
```

```python
import jax
import jax.numpy as jnp
from jax import lax
from jax.experimental import pallas as pl
from jax.experimental.pallas import tpu as pltpu

F32 = jnp.float32
BF16 = jnp.bfloat16

D_MODEL = 1024
HG_HEADS = 4
HG_DK = 64
HG_DV = 64
HG_WIDTH = HG_HEADS * HG_DV
POOL_WINDOWS = (2, 4, 8, 16)
POOL_GDIM = 64
POOL_WIDTH = len(POOL_WINDOWS) * POOL_GDIM
POOL_TAIL = 32
ATT_HEADS = 8
ATT_KV_HEADS = 2
ATT_HDIM = 64
ATT_WIDTH = ATT_HEADS * ATT_HDIM
ATT_KV_WIDTH = ATT_KV_HEADS * ATT_HDIM
WINDOW = 128
ATT_BLOCK = WINDOW
D_FF = 2816
CONV_WIDTH = 3
EPS = 1e-6
LOG2E = 1.4426950408889634
D_IN = 4 * HG_WIDTH + POOL_WIDTH + ATT_WIDTH + 2 * ATT_KV_WIDTH
OFF_HQ = 0
OFF_HF = OFF_HQ + HG_WIDTH
OFF_HI = OFF_HF + HG_WIDTH
OFF_HG = OFF_HI + HG_WIDTH
OFF_PV = OFF_HG + HG_WIDTH
OFF_AQ = OFF_PV + POOL_WIDTH
OFF_AK = OFF_AQ + ATT_WIDTH
OFF_AV = OFF_AK + ATT_KV_WIDTH

HG_CHUNK = 128
HG_SUB = 32
HG_BLOCK = 16
HG_SAFE_EXP = 60.0
MIX_TILE = 512
FFN_TILE = 512
FFN_SUB = 512
FFN_CHUNK = 256
ADA_TILE = 1024
SUBLANES = 8
VMEM_LIMIT = 56 * 1024 * 1024


def _sigmoid(x):
    return 1.0 / (1.0 + jnp.exp(-x))


def _split3(x):
    hi = x.astype(BF16)
    r1 = x - hi.astype(F32)
    mid = r1.astype(BF16)
    lo = (r1 - mid.astype(F32)).astype(BF16)
    return hi, mid, lo


def _dot(a, b):
    return jnp.dot(a, b, preferred_element_type=F32)


def _dot_nt(a, b):
    return lax.dot_general(a, b, (((1,), (1,)), ((), ())), preferred_element_type=F32)


def _dot_tn(a, b):
    return lax.dot_general(a, b, (((0,), (0,)), ((), ())), preferred_element_type=F32)


def _ada_kernel(c_ref, w_ref, b_ref, lbl_ref, mod_ref, lbp_ref):
    c = c_ref[...]
    ca = c * _sigmoid(c)
    acc = None
    w_pieces = _split3(w_ref[0])[:2]
    for piece in _split3(ca):
        for wp in w_pieces:
            t = _dot(piece, wp)
            acc = t if acc is None else acc + t
    mod_ref[0] = acc + b_ref[0]

    logits = lbl_ref[...]
    depth = logits.shape[0]
    mx = jnp.max(logits, axis=0, keepdims=True)
    ex = jnp.exp(logits - mx)
    sm = ex / jnp.sum(ex, axis=0, keepdims=True)
    run = sm[0:1]
    first = run
    for l in range(depth):
        if l > 0:
            run = run + sm[l:l + 1]
        lb = run - first
        lbp_ref[l, 0:1, :] = jnp.log(lb)
        lbp_ref[l, 1:2, :] = jnp.log1p(-lb)
        lbp_ref[l, 2:3, :] = 1.0 - lb


def _ada_call(c, w_ada, b_ada, hg_lb_logits):
    depth, d, n = w_ada.shape
    bsz = c.shape[0]
    hk = hg_lb_logits.shape[1]
    grid = (depth, n // ADA_TILE)
    return pl.pallas_call(
        _ada_kernel,
        grid=grid,
        in_specs=[
            pl.BlockSpec((bsz, d), lambda l, j: (0, 0)),
            pl.BlockSpec((1, d, ADA_TILE), lambda l, j: (l, 0, j)),
            pl.BlockSpec((1, 1, ADA_TILE), lambda l, j: (l, 0, j)),
            pl.BlockSpec((depth, hk), lambda l, j: (0, 0)),
        ],
        out_specs=[
            pl.BlockSpec((1, bsz, ADA_TILE), lambda l, j: (l, 0, j)),
            pl.BlockSpec((depth, 3, hk), lambda l, j: (0, 0, 0)),
        ],
        out_shape=[
            jax.ShapeDtypeStruct((depth, bsz, n), F32),
            jax.ShapeDtypeStruct((depth, 3, hk), F32),
        ],
        compiler_params=pltpu.CompilerParams(
            dimension_semantics=("arbitrary", "arbitrary"), vmem_limit_bytes=VMEM_LIMIT),
        name="adaln_params",
    )(c, w_ada, b_ada.reshape(depth, 1, n), hg_lb_logits)


def _kv_variants(t, low_half):
    sw = pltpu.roll(t, ATT_HDIM, 1)
    return (jnp.where(low_half, t, 0.0).astype(BF16), jnp.where(low_half, 0.0, sw).astype(BF16),
            jnp.where(low_half, sw, 0.0).astype(BF16), jnp.where(low_half, 0.0, t).astype(BF16))


def _mix_kernel(x_ref, mod_ref, gpre_ref, gpost_ref, win_ref, lbp_ref, hgn_ref, poolw_ref,
                pools_ref, sink_ref, wout_ref, eblk_ref, cum_ref,
                o_ref,
                st_ref, kv_ref, ptail_ref, qf_s, b_s, kk_s, vi_s, oa_s):
    s_idx = pl.program_id(1)
    tile = x_ref.shape[1]

    @pl.when(s_idx == 0)
    def _():
        st_ref[...] = jnp.zeros_like(st_ref)
        kv_ref[...] = jnp.zeros_like(kv_ref)
        ptail_ref[0:POOL_TAIL, :] = jnp.zeros((POOL_TAIL, POOL_WIDTH), F32)

    x = x_ref[0]
    sh1 = mod_ref[0, 0:1, :]
    sc1 = mod_ref[0, 1:2, :]
    gt1 = mod_ref[0, 2:3, :]
    ms = jnp.mean(x * x, axis=-1, keepdims=True)
    h = (x * lax.rsqrt(ms + EPS) * gpre_ref[...]) * (1.0 + sc1) + sh1
    proj = _dot(h.astype(BF16), win_ref[0])

    eblk = eblk_ref[...]
    emask = eblk > 0

    q = proj[:, OFF_HQ:OFF_HQ + HG_WIDTH]
    z = proj[:, OFF_HF:OFF_HF + HG_WIDTH]
    vi = proj[:, OFF_HI:OFF_HI + HG_WIDTH]
    gate = proj[:, OFF_HG:OFF_HG + HG_WIDTH]
    qf = q * _sigmoid(q) * (HG_DK ** -0.5)
    log_sig = jnp.minimum(z, 0.0) - jnp.log(1.0 + jnp.exp(-jnp.abs(z)))
    la = lbp_ref[0, 0:1, :]
    lc = lbp_ref[0, 1:2, :] + log_sig
    log_f = jnp.maximum(la, lc) + jnp.log(1.0 + jnp.exp(-jnp.abs(la - lc)))
    kk = lbp_ref[0, 2:3, :] * _sigmoid(-z)
    parts = jnp.concatenate(_split3(log_f), axis=1)
    cums = _dot(cum_ref[...], parts)
    w = HG_WIDTH
    b_loc = cums[:, 0:w] + cums[:, w:2 * w] + cums[:, 2 * w:3 * w]

    sub = HG_SUB
    spread = None
    for i in range(tile // sub):
        bl = b_loc[i * sub:(i + 1) * sub]
        sp = jnp.max(jnp.abs(bl - bl[sub // 2 - 1:sub // 2]))
        spread = sp if spread is None else jnp.maximum(spread, sp)
    factorisable = spread <= HG_SAFE_EXP

    b2 = b_loc * LOG2E
    n_sub = HG_CHUNK // sub
    half = HG_CHUNK // 2
    lane_head = lax.broadcasted_iota(jnp.int32, (HG_CHUNK, HG_WIDTH), 1) // HG_DK
    head_keep = [jnp.where(lane_head == hd, 1.0, 0.0).astype(BF16) for hd in range(HG_HEADS)]

    def stack_heads(a):
        ab = a.astype(BF16)
        return jnp.concatenate([ab * keep for keep in head_keep], axis=0)

    ti = lax.broadcasted_iota(jnp.int32, (HG_HEADS * HG_CHUNK, HG_CHUNK), 0) % HG_CHUNK
    tj = lax.broadcasted_iota(jnp.int32, (HG_HEADS * HG_CHUNK, HG_CHUNK), 1)
    m_same32 = (ti // sub == tj // sub) & (ti >= tj)
    m_same64 = (ti // (2 * sub) == tj // (2 * sub)) & (ti // sub > tj // sub)
    zsub = jnp.zeros((sub, HG_WIDTH), F32)
    zhalf = jnp.zeros((half, HG_WIDTH), F32)
    st = st_ref[...]
    oa_chunks = []
    for c in range(tile // HG_CHUNK):
        r = slice(c * HG_CHUNK, (c + 1) * HG_CHUNK)
        bl = b2[r]
        qc = qf[r]
        kc = kk[r]
        vc = vi[r]
        blk = lambda a, i: a[i * sub:(i + 1) * sub]
        row = lambda i: bl[i:i + 1]
        mids = [row(i * sub + sub // 2 - 1) for i in range(n_sub)]
        q_a = jnp.concatenate([blk(qc, i) * jnp.exp2(blk(bl, i) - mids[i]) for i in range(n_sub)], axis=0)
        k_a = jnp.concatenate([blk(kc, i) * jnp.exp2(mids[i] - blk(bl, i)) for i in range(n_sub)], axis=0)
        q_b = jnp.concatenate(
            [zsub if i % 2 == 0 else blk(qc, i) * jnp.exp2(blk(bl, i) - row(i * sub - 1)) for i in range(n_sub)],
            axis=0)
        k_b = jnp.concatenate(
            [blk(kc, i) * jnp.exp2(row((i + 1) * sub - 1) - blk(bl, i)) if i % 2 == 0 else zsub
             for i in range(n_sub)], axis=0)
        q_c = jnp.concatenate([zhalf, qc[half:] * jnp.exp2(bl[half:] - row(half - 1))], axis=0)
        k_c = jnp.concatenate([kc[:half] * jnp.exp2(row(half - 1) - bl[:half]), zhalf], axis=0)
        s_a = _dot_nt(stack_heads(q_a), k_a.astype(BF16))
        s_b = _dot_nt(stack_heads(q_b), k_b.astype(BF16))
        s_c = _dot_nt(stack_heads(q_c), k_c.astype(BF16))
        p = jnp.where(m_same32, s_a, jnp.where(m_same64, s_b, s_c)).astype(BF16)
        pcat = jnp.concatenate([p[hd * HG_CHUNK:(hd + 1) * HG_CHUNK] for hd in range(HG_HEADS)], axis=1)
        o_intra = _dot(pcat, stack_heads(vc))
        b_end = row(HG_CHUNK - 1)
        qs = (qc * jnp.exp2(bl)).astype(BF16)
        ke = (kc * jnp.exp2(b_end - bl)).astype(BF16)
        oa_chunks.append(o_intra + _dot_nt(qs, st.astype(BF16)))
        st = st * jnp.exp2(b_end) + jnp.where(emask, _dot_tn(vc.astype(BF16), ke), 0.0)
    oa_fast = jnp.concatenate(oa_chunks, axis=0)
    st_fast = st

    pv = proj[:, OFF_PV:OFF_PV + POOL_WIDTH]
    ptail_ref[POOL_TAIL:POOL_TAIL + tile, :] = pv
    lane_p = lax.broadcasted_iota(jnp.int32, (1, POOL_WIDTH), 1)
    win_l = jnp.left_shift(2, lane_p // POOL_GDIM)
    groups_per_col = 128 // POOL_GDIM
    first_group = lax.broadcasted_iota(jnp.int32, (tile, 128), 1) < POOL_GDIM
    cols = []
    for col in range(POOL_WIDTH // 128):
        cur = ptail_ref[:, col * 128:(col + 1) * 128]
        start = 0
        sums = []
        for level in range((col + 1) * groups_per_col):
            shift = 1 << level
            rows = cur.shape[0]
            cur = cur[SUBLANES:] + cur[SUBLANES - shift:rows - shift]
            start += SUBLANES
            sums.append(cur[POOL_TAIL - start:])
        cols.append(jnp.where(first_group, sums[-2], sums[-1]))
    acc_p = jnp.concatenate(cols, axis=1)
    t_glob = s_idx * tile + lax.broadcasted_iota(jnp.int32, (tile, POOL_WIDTH), 0)
    cnt = jnp.minimum(t_glob + 1, win_l).astype(F32)
    pooled = acc_p / cnt - pv
    ptail_ref[0:POOL_TAIL, :] = ptail_ref[pl.ds(tile, POOL_TAIL), :]
    o_b = _dot(pooled.astype(BF16), poolw_ref[...]) * pools_ref[...]

    group = ATT_HEADS // ATT_KV_HEADS
    pair_rows = (group // 2) * ATT_BLOCK
    low_half = lax.broadcasted_iota(jnp.int32, (ATT_BLOCK, ATT_KV_WIDTH), 1) < ATT_HDIM
    qi = lax.broadcasted_iota(jnp.int32, (pair_rows, ATT_BLOCK), 0) % ATT_BLOCK
    kj = lax.broadcasted_iota(jnp.int32, (pair_rows, ATT_BLOCK), 1)
    upper = kj > qi
    first_pair = lax.broadcasted_iota(jnp.int32, (pair_rows, 1), 0) < ATT_BLOCK
    q_scale = (ATT_HDIM ** -0.5) * LOG2E
    prev = None
    o_c_blocks = []
    for a in range(tile // ATT_BLOCK):
        r = slice(a * ATT_BLOCK, (a + 1) * ATT_BLOCK)
        cur_k = _kv_variants(proj[r, OFF_AK:OFF_AK + ATT_KV_WIDTH], low_half)
        cur_v = _kv_variants(proj[r, OFF_AV:OFF_AV + ATT_KV_WIDTH], low_half)
        if prev is None:
            prev_k = tuple(kv_ref[n] for n in range(4))
            prev_v = tuple(kv_ref[4 + n] for n in range(4))
            no_prev = jnp.where(upper & (s_idx == 0), -jnp.inf, 0.0)
        else:
            prev_k, prev_v = prev
            no_prev = None
        pair_out = []
        for g in range(ATT_KV_HEADS):
            c0 = OFF_AQ + g * group * ATT_HDIM
            q2 = jnp.concatenate(
                [(proj[r, c0 + p * 2 * ATT_HDIM:c0 + (p + 1) * 2 * ATT_HDIM] * q_scale).astype(BF16)
                 for p in range(group // 2)], axis=0)
            kcat = jnp.concatenate([prev_k[2 * g], cur_k[2 * g], prev_k[2 * g + 1], cur_k[2 * g + 1]], axis=0)
            s_all = _dot_nt(q2, kcat)
            o_g = None
            for half in range(2):
                s_prev = s_all[:, (2 * half) * ATT_BLOCK:(2 * half + 1) * ATT_BLOCK]
                s_cur = s_all[:, (2 * half + 1) * ATT_BLOCK:(2 * half + 2) * ATT_BLOCK]
                if no_prev is not None:
                    s_prev = s_prev + no_prev
                s = jnp.where(upper, s_prev, s_cur)
                head0 = g * group + half
                sink = jnp.where(first_pair, sink_ref[head0], sink_ref[head0 + 2]) * LOG2E
                m = jnp.maximum(jnp.max(s, axis=-1, keepdims=True), sink)
                pexp = jnp.exp2(s - m)
                denom = jnp.sum(pexp, axis=-1, keepdims=True) + jnp.exp2(sink - m)
                pcat = jnp.concatenate([jnp.where(upper, pexp, 0.0).astype(BF16),
                                        jnp.where(upper, 0.0, pexp).astype(BF16)], axis=1)
                vcat = jnp.concatenate([prev_v[2 * g + half], cur_v[2 * g + half]], axis=0)
                o_h = _dot(pcat, vcat) * (1.0 / denom)
                o_g = o_h if o_g is None else o_g + o_h
            for p in range(group // 2):
                pair_out.append(o_g[p * ATT_BLOCK:(p + 1) * ATT_BLOCK])
        o_c_blocks.append(jnp.concatenate(pair_out, axis=1))
        prev = (cur_k, cur_v)
    for n in range(4):
        kv_ref[n] = prev[0][n]
        kv_ref[4 + n] = prev[1][n]
    o_c = jnp.concatenate(o_c_blocks, axis=0)

    mix_bc = _dot(jnp.concatenate([o_b.astype(BF16), o_c.astype(BF16)], axis=1), wout_ref[0, HG_WIDTH:, :])

    oa_s[...] = oa_fast

    @pl.when(factorisable)
    def _():
        st_ref[...] = st_fast

    @pl.when(jnp.logical_not(factorisable))
    def _():
        qf_s[...] = qf
        b_s[...] = b_loc
        kk_s[...] = kk
        vi_s[...] = vi
        rows16 = lax.broadcasted_iota(jnp.int32, (HG_BLOCK, HG_WIDTH), 0)
        per_chunk = HG_CHUNK // HG_BLOCK

        def hg_step(i, carry):
            r0 = pl.multiple_of(i * HG_BLOCK, HG_BLOCK)
            qb = qf_s[pl.ds(r0, HG_BLOCK), :]
            bb = b_s[pl.ds(r0, HG_BLOCK), :]
            kb = kk_s[pl.ds(r0, HG_BLOCK), :]
            b_prev = jnp.where(i % per_chunk == 0, 0.0, b_s[pl.ds(jnp.maximum(r0 - 1, 0), 1), :])
            b_end = b_s[pl.ds(r0 + HG_BLOCK - 1, 1), :]
            pieces = []
            for j in range(HG_BLOCK):
                bj = b_s[pl.ds(r0 + j, 1), :]
                kj_row = kk_s[pl.ds(r0 + j, 1), :]
                d = jnp.where(rows16 >= j, (qb * kj_row) * jnp.exp(bb - bj), 0.0)
                pieces.append(d.astype(BF16))
            dall = jnp.concatenate(pieces, axis=0)
            sall = _dot(dall, eblk)
            acc = jnp.zeros((HG_BLOCK, HG_WIDTH), F32)
            for j in range(HG_BLOCK):
                vj = vi_s[pl.ds(r0 + j, 1), :]
                acc = acc + sall[j * HG_BLOCK:(j + 1) * HG_BLOCK, :] * vj
            st_old = st_ref[...]
            qs_b = (qb * jnp.exp(bb - b_prev)).astype(BF16)
            ke_b = (kb * jnp.exp(b_end - bb)).astype(BF16)
            oa_s[pl.ds(r0, HG_BLOCK), :] = acc + _dot_nt(qs_b, st_old.astype(BF16))
            vb_b = vi_s[pl.ds(r0, HG_BLOCK), :].astype(BF16)
            st_ref[...] = st_old * jnp.exp(b_end - b_prev) + jnp.where(emask, _dot_tn(vb_b, ke_b), 0.0)
            return carry

        lax.fori_loop(0, tile // HG_BLOCK, hg_step, 0)

    oa = oa_s[...]
    sq_hi, sq_lo, _ = _split3(oa * oa)
    ms_h = (_dot(sq_hi, eblk) + _dot(sq_lo, eblk)) * (1.0 / HG_DV)
    o_a = oa * lax.rsqrt(ms_h + EPS) * hgn_ref[...] * (gate * _sigmoid(gate))

    mix = mix_bc + _dot(o_a.astype(BF16), wout_ref[0, 0:HG_WIDTH, :])
    msm = jnp.mean(mix * mix, axis=-1, keepdims=True)
    o_ref[0] = x + gt1 * (mix * lax.rsqrt(msm + EPS) * gpost_ref[...])


def _const_spec(shape):
    nd = len(shape)
    return pl.BlockSpec(shape, lambda b, s, _nd=nd: (0,) * _nd, pipeline_mode=pl.Buffered(1))


def _layer_spec(shape, layer):
    nd = len(shape)
    return pl.BlockSpec((1,) + tuple(shape[1:]), lambda b, s, _nd=nd, _l=layer: (_l,) + (0,) * (_nd - 1),
                        pipeline_mode=pl.Buffered(1))


def _mix_call(layer, x, mod, g_pre, g_post, w_in, lbp, hg_norm_t, pool_bd, pool_scale, sinks, w_out, eblk, cum):
    bsz, seq, d = x.shape
    tile = min(MIX_TILE, seq)
    grid = (bsz, seq // tile)
    row = lambda v: v.reshape(1, -1)
    in_specs = [
        pl.BlockSpec((1, tile, d), lambda b, s: (b, s, 0)),
        pl.BlockSpec((1, 6, d), lambda b, s: (b, 0, 0)),
        _const_spec((1, d)), _const_spec((1, d)),
        _layer_spec(w_in.shape, layer),
        _layer_spec(lbp.shape, layer),
        _const_spec((1, HG_WIDTH)),
        _const_spec(pool_bd.shape),
        _const_spec((1, POOL_WIDTH)),
        pl.BlockSpec(memory_space=pltpu.SMEM),
        _layer_spec(w_out.shape, layer),
        _const_spec(eblk.shape),
        _const_spec(cum.shape),
    ]
    scratch = [
        pltpu.VMEM((HG_WIDTH, HG_WIDTH), F32),
        pltpu.VMEM((8, ATT_BLOCK, ATT_KV_WIDTH), BF16),
        pltpu.VMEM((POOL_TAIL + tile, POOL_WIDTH), F32),
        pltpu.VMEM((tile, HG_WIDTH), F32),
        pltpu.VMEM((tile, HG_WIDTH), F32),
        pltpu.VMEM((tile, HG_WIDTH), F32),
        pltpu.VMEM((tile, HG_WIDTH), F32),
        pltpu.VMEM((tile, HG_WIDTH), F32),
    ]
    return pl.pallas_call(
        _mix_kernel,
        grid=grid,
        in_specs=in_specs,
        out_specs=pl.BlockSpec((1, tile, d), lambda b, s: (b, s, 0)),
        out_shape=jax.ShapeDtypeStruct(x.shape, F32),
        scratch_shapes=scratch,
        compiler_params=pltpu.CompilerParams(
            dimension_semantics=("arbitrary", "arbitrary"), vmem_limit_bytes=VMEM_LIMIT),
        name="token_mixer",
    )(x, mod, row(g_pre), row(g_post), w_in, lbp, row(hg_norm_t), pool_bd, row(pool_scale), sinks,
      w_out, eblk, cum)


def _gelu_tanh(x):
    c = 0.7978845608028654
    return (0.5 * x) * (1.0 + jnp.tanh(x * (c + (c * 0.044715) * (x * x))))


def _ffn_kernel(x_ref, mod_ref, gpre_ref, gpost_ref, wup_ref, cw_ref, cb_ref, wdown_ref,
                o_ref,
                tail_ref, ubuf_ref, act_ref):
    s_idx = pl.program_id(1)
    tile = x_ref.shape[1]
    sub = min(FFN_SUB, tile)
    n_sub = tile // sub

    @pl.when(s_idx == 0)
    def _():
        tail_ref[...] = jnp.zeros_like(tail_ref)

    sh2 = mod_ref[0, 3:4, :]
    gain = gpre_ref[...] * (1.0 + mod_ref[0, 4:5, :])
    gt2 = mod_ref[0, 5:6, :]

    hbs = []
    for i in range(n_sub):
        x = x_ref[0, i * sub:(i + 1) * sub, :]
        ms = jnp.mean(x * x, axis=-1, keepdims=True)
        hbs.append(((x * lax.rsqrt(ms + EPS)) * gain + sh2).astype(BF16))

    def conv_cols(hb, off, slot):
        ubuf = ubuf_ref.at[slot]
        u = _dot(hb, wup_ref[0, :, off:off + FFN_CHUNK])
        ubuf[0:SUBLANES, :] = tail_ref[:, off:off + FFN_CHUNK]
        ubuf[SUBLANES:SUBLANES + sub, :] = u
        tail_ref[:, off:off + FFN_CHUNK] = ubuf[pl.ds(sub, SUBLANES), :]
        w0 = cw_ref[0:1, off:off + FFN_CHUNK]
        w1 = cw_ref[1:2, off:off + FFN_CHUNK]
        w2 = cw_ref[2:3, off:off + FFN_CHUNK]
        return (w0 * ubuf[pl.ds(SUBLANES - 2, sub), :] + w1 * ubuf[pl.ds(SUBLANES - 1, sub), :]
                + w2 * u + cb_ref[0:1, off:off + FFN_CHUNK])

    for c in range(D_FF // FFN_CHUNK):
        for i in range(n_sub):
            slot = 4 * i + 2 * (c % 2)
            g = conv_cols(hbs[i], c * FFN_CHUNK, slot)
            v = conv_cols(hbs[i], D_FF + c * FFN_CHUNK, slot + 1)
            act_ref[i * sub:(i + 1) * sub, c * FFN_CHUNK:(c + 1) * FFN_CHUNK] = (_gelu_tanh(g) * v).astype(BF16)

    for i in range(n_sub):
        y = _dot(act_ref[i * sub:(i + 1) * sub, :], wdown_ref[0])
        msy = jnp.mean(y * y, axis=-1, keepdims=True)
        x = x_ref[0, i * sub:(i + 1) * sub, :]
        o_ref[0, i * sub:(i + 1) * sub, :] = x + gt2 * (y * lax.rsqrt(msy + EPS) * gpost_ref[...])


def _ffn_call(layer, x, mod, g_pre, g_post, w_up, conv_w, conv_b, w_down):
    bsz, seq, d = x.shape
    tile = min(FFN_TILE, seq)
    grid = (bsz, seq // tile)
    row = lambda v: v.reshape(1, -1)
    in_specs = [
        pl.BlockSpec((1, tile, d), lambda b, s: (b, s, 0)),
        pl.BlockSpec((1, 6, d), lambda b, s: (b, 0, 0)),
        _const_spec((1, d)), _const_spec((1, d)),
        _layer_spec(w_up.shape, layer),
        _const_spec(conv_w.shape),
        _const_spec((1, 2 * D_FF)),
        _layer_spec(w_down.shape, layer),
    ]
    scratch = [
        pltpu.VMEM((SUBLANES, 2 * D_FF), F32),
        pltpu.VMEM((4 * (tile // min(FFN_SUB, tile)), SUBLANES + min(FFN_SUB, tile), FFN_CHUNK), F32),
        pltpu.VMEM((tile, D_FF), BF16),
    ]
    return pl.pallas_call(
        _ffn_kernel,
        grid=grid,
        in_specs=in_specs,
        out_specs=pl.BlockSpec((1, tile, d), lambda b, s: (b, s, 0)),
        out_shape=jax.ShapeDtypeStruct(x.shape, F32),
        scratch_shapes=scratch,
        compiler_params=pltpu.CompilerParams(
            dimension_semantics=("arbitrary", "arbitrary"), vmem_limit_bytes=VMEM_LIMIT),
        name="conv_ffn",
    )(x, mod, row(g_pre), row(g_post), w_up, conv_w, row(conv_b), w_down)


def _block_diag(blocks):
    g, c, _ = blocks.shape
    eye = jnp.eye(g, dtype=blocks.dtype)
    return (eye[:, None, :, None] * blocks[:, :, None, :]).reshape(g * c, g * c)


def _cum_matrix(tile):
    r = jnp.arange(tile)
    same = (r[:, None] // HG_CHUNK) == (r[None, :] // HG_CHUNK)
    return (same & (r[None, :] <= r[:, None])).astype(BF16)


def kernel(x, c, w_ada, b_ada, g_pre_mix, g_post_mix, w_in, hg_lb_logits, hg_norm, pool_w, pool_scale,
           attn_sinks, w_out, g_pre_ffn, g_post_ffn, w_up, conv_w, conv_b, w_down):
    depth = w_ada.shape[0]
    bsz, seq, d = x.shape
    mod_all, lbp_all = _ada_call(c, w_ada, b_ada, hg_lb_logits)
    mod_all = mod_all.reshape(depth, bsz, 6, d)
    eblk = _block_diag(jnp.ones((HG_HEADS, HG_DK, HG_DV), BF16))
    cum = _cum_matrix(min(MIX_TILE, seq))
    w_in_b, w_out_b, w_up_b, w_down_b = (t.astype(BF16) for t in (w_in, w_out, w_up, w_down))
    for l in range(depth):
        x = _mix_call(
            l, x, mod_all[l], g_pre_mix[l], g_post_mix[l], w_in_b, lbp_all,
            jnp.tile(hg_norm[l], HG_HEADS), _block_diag(pool_w[l]).astype(BF16), pool_scale[l],
            attn_sinks[l], w_out_b, eblk, cum)
        x = _ffn_call(
            l, x, mod_all[l], g_pre_ffn[l], g_post_ffn[l], w_up_b, conv_w[l], conv_b[l], w_down_b)
    return x
```

```python
import jax
import jax.numpy as jnp
from jax import lax
from jax.experimental import pallas as pl
from jax.experimental.pallas import tpu as pltpu

F32 = jnp.float32
BF16 = jnp.bfloat16

D_MODEL = 1024
HG_HEADS = 4
HG_DK = 64
HG_DV = 64
HG_WIDTH = HG_HEADS * HG_DV
POOL_WINDOWS = (2, 4, 8, 16)
POOL_GDIM = 64
POOL_WIDTH = len(POOL_WINDOWS) * POOL_GDIM
POOL_TAIL = 32
ATT_HEADS = 8
ATT_KV_HEADS = 2
ATT_HDIM = 64
ATT_WIDTH = ATT_HEADS * ATT_HDIM
ATT_KV_WIDTH = ATT_KV_HEADS * ATT_HDIM
WINDOW = 128
ATT_BLOCK = WINDOW
D_FF = 2816
CONV_WIDTH = 3
EPS = 1e-6
LOG2E = 1.4426950408889634
D_IN = 4 * HG_WIDTH + POOL_WIDTH + ATT_WIDTH + 2 * ATT_KV_WIDTH
OFF_HQ = 0
OFF_HF = OFF_HQ + HG_WIDTH
OFF_HI = OFF_HF + HG_WIDTH
OFF_HG = OFF_HI + HG_WIDTH
OFF_PV = OFF_HG + HG_WIDTH
OFF_AQ = OFF_PV + POOL_WIDTH
OFF_AK = OFF_AQ + ATT_WIDTH
OFF_AV = OFF_AK + ATT_KV_WIDTH

HG_CHUNK = 128
HG_SUB = 32
HG_BLOCK = 16
HG_SAFE_EXP = 60.0
MIX_TILE = 512
PROJ_GROUP = 512
FFN_TILE = 512
FFN_CHUNK = 256
FFN_EARLY_DOWN = 2
ADA_TILE = 1024
SUBLANES = 8
VMEM_LIMIT = 56 * 1024 * 1024


def _sigmoid(x):
    return 1.0 / (1.0 + jnp.exp(-x))


def _split3(x):
    hi = x.astype(BF16)
    r1 = x - hi.astype(F32)
    mid = r1.astype(BF16)
    lo = (r1 - mid.astype(F32)).astype(BF16)
    return hi, mid, lo


def _dot(a, b):
    return jnp.dot(a, b, preferred_element_type=F32)


def _dot_nt(a, b):
    return lax.dot_general(a, b, (((1,), (1,)), ((), ())), preferred_element_type=F32)


def _dot_tn(a, b):
    return lax.dot_general(a, b, (((0,), (0,)), ((), ())), preferred_element_type=F32)


def _ada_kernel(c_ref, w_ref, b_ref, lbl_ref, mod_ref, lbp_ref):
    c = c_ref[...]
    ca = c * _sigmoid(c)
    acc = None
    wb = w_ref[0].astype(BF16)
    for piece in _split3(ca):
        t = _dot(piece, wb)
        acc = t if acc is None else acc + t
    mod_ref[0] = acc + b_ref[0]

    logits = lbl_ref[...]
    depth = logits.shape[0]
    mx = jnp.max(logits, axis=0, keepdims=True)
    ex = jnp.exp(logits - mx)
    sm = ex / jnp.sum(ex, axis=0, keepdims=True)
    run = sm[0:1]
    first = run
    for l in range(depth):
        if l > 0:
            run = run + sm[l:l + 1]
        lb = run - first
        lbp_ref[l, 0:1, :] = jnp.log(lb)
        lbp_ref[l, 1:2, :] = jnp.log1p(-lb)
        lbp_ref[l, 2:3, :] = 1.0 - lb


def _ada_call(c, w_ada, b_ada, hg_lb_logits):
    depth, d, n = w_ada.shape
    bsz = c.shape[0]
    hk = hg_lb_logits.shape[1]
    grid = (depth, n // ADA_TILE)
    return pl.pallas_call(
        _ada_kernel,
        grid=grid,
        in_specs=[
            pl.BlockSpec((bsz, d), lambda l, j: (0, 0)),
            pl.BlockSpec((1, d, ADA_TILE), lambda l, j: (l, 0, j)),
            pl.BlockSpec((1, 1, ADA_TILE), lambda l, j: (l, 0, j)),
            pl.BlockSpec((depth, hk), lambda l, j: (0, 0)),
        ],
        out_specs=[
            pl.BlockSpec((1, bsz, ADA_TILE), lambda l, j: (l, 0, j)),
            pl.BlockSpec((depth, 3, hk), lambda l, j: (0, 0, 0)),
        ],
        out_shape=[
            jax.ShapeDtypeStruct((depth, bsz, n), F32),
            jax.ShapeDtypeStruct((depth, 3, hk), F32),
        ],
        compiler_params=pltpu.CompilerParams(
            dimension_semantics=("arbitrary", "arbitrary"), vmem_limit_bytes=VMEM_LIMIT),
        name="adaln_params",
    )(c, w_ada, b_ada.reshape(depth, 1, n), hg_lb_logits)


def _kv_variants(t, low_half):
    sw = pltpu.roll(t, ATT_HDIM, 1)
    return (jnp.where(low_half, t, 0.0).astype(BF16), jnp.where(low_half, 0.0, sw).astype(BF16),
            jnp.where(low_half, sw, 0.0).astype(BF16), jnp.where(low_half, 0.0, t).astype(BF16))


class _Columns:
    def __init__(self, group_fn, width):
        self.group_fn, self.width, self.pieces = group_fn, width, {}

    def __getitem__(self, idx):
        rows, cols = idx
        piece, lo = divmod(cols.start, self.width)
        assert cols.stop - cols.start <= self.width - lo, "column slice straddles two groups"
        if piece not in self.pieces:
            self.pieces[piece] = self.group_fn(piece * self.width)
        return self.pieces[piece][rows, lo:lo + cols.stop - cols.start]


def _mix_kernel(x_ref, mod_ref, gpre_ref, gpost_ref, win32_ref, lbp_ref, hgn_ref, poolw_ref,
                pools_ref, sink_ref, wout32_ref, eblk_ref, cum_ref,
                o_ref,
                win_ref, wout_ref, st_ref, kv_ref, ptail_ref, qf_s, b_s, kk_s, vi_s, oa_s):
    s_idx = pl.program_id(1)
    tile = x_ref.shape[1]

    @pl.when((pl.program_id(0) == 0) & (s_idx == 0))
    def _():
        win_ref[...] = win32_ref[...].astype(BF16)
        wout_ref[...] = wout32_ref[...].astype(BF16)

    @pl.when(s_idx == 0)
    def _():
        st_ref[...] = jnp.zeros_like(st_ref)
        kv_ref[...] = jnp.zeros_like(kv_ref)
        ptail_ref[0:POOL_TAIL, :] = jnp.zeros((POOL_TAIL, POOL_WIDTH), F32)

    x = x_ref[0]
    gt1 = mod_ref[0, 2:3, :]
    ms = jnp.mean(x * x, axis=-1, keepdims=True)
    hb = ((x * lax.rsqrt(ms + EPS)) * (gpre_ref[...] * (1.0 + mod_ref[0, 1:2, :])) + mod_ref[0, 0:1, :]).astype(BF16)
    proj = _Columns(lambda c: _dot(hb, win_ref[0, :, c:c + PROJ_GROUP]), PROJ_GROUP)

    eblk = eblk_ref[...]
    emask = eblk > 0

    q = proj[:, OFF_HQ:OFF_HQ + HG_WIDTH]
    z = proj[:, OFF_HF:OFF_HF + HG_WIDTH]
    qf = q * _sigmoid(q) * (HG_DK ** -0.5)
    log_sig = jnp.minimum(z, 0.0) - jnp.log(1.0 + jnp.exp(-jnp.abs(z)))
    la = lbp_ref[0, 0:1, :]
    lc = lbp_ref[0, 1:2, :] + log_sig
    log_f = jnp.maximum(la, lc) + jnp.log(1.0 + jnp.exp(-jnp.abs(la - lc)))
    kk = lbp_ref[0, 2:3, :] * _sigmoid(-z)
    vi = proj[:, OFF_HI:OFF_HI + HG_WIDTH]
    gate = proj[:, OFF_HG:OFF_HG + HG_WIDTH]
    out_gain = hgn_ref[...] * (gate * _sigmoid(gate))
    parts = jnp.concatenate(_split3(log_f), axis=1)
    cums = _dot(cum_ref[...], parts)
    w = HG_WIDTH
    b_loc = cums[:, 0:w] + cums[:, w:2 * w] + cums[:, 2 * w:3 * w]

    sub = HG_SUB
    spread = None
    for i in range(tile // sub):
        bl = b_loc[i * sub:(i + 1) * sub]
        sp = jnp.max(jnp.abs(bl - bl[sub // 2 - 1:sub // 2]))
        spread = sp if spread is None else jnp.maximum(spread, sp)
    factorisable = spread <= HG_SAFE_EXP

    def pairwise_path():
        qf_s[...] = qf
        b_s[...] = b_loc
        kk_s[...] = kk
        vi_s[...] = vi
        rows16 = lax.broadcasted_iota(jnp.int32, (HG_BLOCK, HG_WIDTH), 0)
        per_chunk = HG_CHUNK // HG_BLOCK

        def hg_step(i, carry):
            r0 = pl.multiple_of(i * HG_BLOCK, HG_BLOCK)
            qb = qf_s[pl.ds(r0, HG_BLOCK), :]
            bb = b_s[pl.ds(r0, HG_BLOCK), :]
            kb = kk_s[pl.ds(r0, HG_BLOCK), :]
            b_prev = jnp.where(i % per_chunk == 0, 0.0, b_s[pl.ds(jnp.maximum(r0 - 1, 0), 1), :])
            b_end = b_s[pl.ds(r0 + HG_BLOCK - 1, 1), :]
            pieces = []
            for j in range(HG_BLOCK):
                bj = b_s[pl.ds(r0 + j, 1), :]
                kj_row = kk_s[pl.ds(r0 + j, 1), :]
                d = jnp.where(rows16 >= j, (qb * kj_row) * jnp.exp(bb - bj), 0.0)
                pieces.append(d.astype(BF16))
            dall = jnp.concatenate(pieces, axis=0)
            sall = _dot(dall, eblk)
            acc = jnp.zeros((HG_BLOCK, HG_WIDTH), F32)
            for j in range(HG_BLOCK):
                vj = vi_s[pl.ds(r0 + j, 1), :]
                acc = acc + sall[j * HG_BLOCK:(j + 1) * HG_BLOCK, :] * vj
            st_old = st_ref[...]
            qs_b = (qb * jnp.exp(bb - b_prev)).astype(BF16)
            ke_b = (kb * jnp.exp(b_end - bb)).astype(BF16)
            oa_s[pl.ds(r0, HG_BLOCK), :] = acc + _dot_nt(qs_b, st_old.astype(BF16))
            vb_b = vi_s[pl.ds(r0, HG_BLOCK), :].astype(BF16)
            st_ref[...] = st_old * jnp.exp(b_end - b_prev) + jnp.where(emask, _dot_tn(vb_b, ke_b), 0.0)
            return carry

        lax.fori_loop(0, tile // HG_BLOCK, hg_step, 0)

    b2 = b_loc * LOG2E
    n_sub = HG_CHUNK // sub
    half = HG_CHUNK // 2
    lane_head = lax.broadcasted_iota(jnp.int32, (HG_CHUNK, HG_WIDTH), 1) // HG_DK
    head_keep = [jnp.where(lane_head == hd, 1.0, 0.0).astype(BF16) for hd in range(HG_HEADS)]

    def stack_heads(a):
        ab = a.astype(BF16)
        return jnp.concatenate([ab * keep for keep in head_keep], axis=0)

    ti = lax.broadcasted_iota(jnp.int32, (HG_HEADS * HG_CHUNK, HG_CHUNK), 0) % HG_CHUNK
    tj = lax.broadcasted_iota(jnp.int32, (HG_HEADS * HG_CHUNK, HG_CHUNK), 1)
    m_same32 = (ti // sub == tj // sub) & (ti >= tj)
    m_same64 = (ti // (2 * sub) == tj // (2 * sub)) & (ti // sub > tj // sub)
    zsub = jnp.zeros((sub, HG_WIDTH), F32)
    zhalf = jnp.zeros((half, HG_WIDTH), F32)
    def hg_chunk(c, st):
        r = slice(c * HG_CHUNK, (c + 1) * HG_CHUNK)
        bl = b2[r]
        qc = qf[r]
        kc = kk[r]
        vc = vi[r]
        blk = lambda a, i: a[i * sub:(i + 1) * sub]
        row = lambda i: bl[i:i + 1]
        mids = [row(i * sub + sub // 2 - 1) for i in range(n_sub)]
        q_a = jnp.concatenate([blk(qc, i) * jnp.exp2(blk(bl, i) - mids[i]) for i in range(n_sub)], axis=0)
        k_a = jnp.concatenate([blk(kc, i) * jnp.exp2(mids[i] - blk(bl, i)) for i in range(n_sub)], axis=0)
        q_b = jnp.concatenate(
            [zsub if i % 2 == 0 else blk(qc, i) * jnp.exp2(blk(bl, i) - row(i * sub - 1)) for i in range(n_sub)],
            axis=0)
        k_b = jnp.concatenate(
            [blk(kc, i) * jnp.exp2(row((i + 1) * sub - 1) - blk(bl, i)) if i % 2 == 0 else zsub
             for i in range(n_sub)], axis=0)
        q_c = jnp.concatenate([zhalf, qc[half:] * jnp.exp2(bl[half:] - row(half - 1))], axis=0)
        k_c = jnp.concatenate([kc[:half] * jnp.exp2(row(half - 1) - bl[:half]), zhalf], axis=0)
        s_a = _dot_nt(stack_heads(q_a), k_a.astype(BF16))
        s_b = _dot_nt(stack_heads(q_b), k_b.astype(BF16))
        s_c = _dot_nt(stack_heads(q_c), k_c.astype(BF16))
        p = jnp.where(m_same32, s_a, jnp.where(m_same64, s_b, s_c)).astype(BF16)
        pcat = jnp.concatenate([p[hd * HG_CHUNK:(hd + 1) * HG_CHUNK] for hd in range(HG_HEADS)], axis=1)
        o_intra = _dot(pcat, stack_heads(vc))
        b_end = row(HG_CHUNK - 1)
        qs = (qc * jnp.exp2(bl)).astype(BF16)
        ke = (kc * jnp.exp2(b_end - bl)).astype(BF16)
        o_chunk = o_intra + _dot_nt(qs, st.astype(BF16))
        return o_chunk, st * jnp.exp2(b_end) + jnp.where(emask, _dot_tn(vc.astype(BF16), ke), 0.0)

    pv = proj[:, OFF_PV:OFF_PV + POOL_WIDTH]
    ptail_ref[POOL_TAIL:POOL_TAIL + tile, :] = pv
    lane_p = lax.broadcasted_iota(jnp.int32, (1, POOL_WIDTH), 1)
    win_l = jnp.left_shift(2, lane_p // POOL_GDIM)
    groups_per_col = 128 // POOL_GDIM
    first_group = lax.broadcasted_iota(jnp.int32, (tile, 128), 1) < POOL_GDIM
    cols = []
    for col in range(POOL_WIDTH // 128):
        cur = ptail_ref[:, col * 128:(col + 1) * 128]
        start = 0
        sums = []
        for level in range((col + 1) * groups_per_col):
            shift = 1 << level
            rows = cur.shape[0]
            cur = cur[SUBLANES:] + cur[SUBLANES - shift:rows - shift]
            start += SUBLANES
            sums.append(cur[POOL_TAIL - start:])
        cols.append(jnp.where(first_group, sums[-2], sums[-1]))
    acc_p = jnp.concatenate(cols, axis=1)
    t_glob = s_idx * tile + lax.broadcasted_iota(jnp.int32, (tile, POOL_WIDTH), 0)
    cnt = jnp.minimum(t_glob + 1, win_l).astype(F32)
    pooled = acc_p / cnt - pv
    ptail_ref[0:POOL_TAIL, :] = ptail_ref[pl.ds(tile, POOL_TAIL), :]
    o_b = _dot(pooled.astype(BF16), poolw_ref[...]) * pools_ref[...]

    group = ATT_HEADS // ATT_KV_HEADS
    pair_rows = (group // 2) * ATT_BLOCK
    low_half = lax.broadcasted_iota(jnp.int32, (ATT_BLOCK, ATT_KV_WIDTH), 1) < ATT_HDIM
    qi = lax.broadcasted_iota(jnp.int32, (pair_rows, ATT_BLOCK), 0) % ATT_BLOCK
    kj = lax.broadcasted_iota(jnp.int32, (pair_rows, ATT_BLOCK), 1)
    upper = kj > qi
    first_pair = lax.broadcasted_iota(jnp.int32, (pair_rows, 1), 0) < ATT_BLOCK
    q_scale = (ATT_HDIM ** -0.5) * LOG2E
    def att_block(a, prev):
        r = slice(a * ATT_BLOCK, (a + 1) * ATT_BLOCK)
        cur_k = _kv_variants(proj[r, OFF_AK:OFF_AK + ATT_KV_WIDTH], low_half)
        cur_v = _kv_variants(proj[r, OFF_AV:OFF_AV + ATT_KV_WIDTH], low_half)
        if prev is None:
            prev_k = tuple(kv_ref[n] for n in range(4))
            prev_v = tuple(kv_ref[4 + n] for n in range(4))
            no_prev = jnp.where(upper & (s_idx == 0), -jnp.inf, 0.0)
        else:
            prev_k, prev_v = prev
            no_prev = None
        pair_out = []
        for g in range(ATT_KV_HEADS):
            c0 = OFF_AQ + g * group * ATT_HDIM
            q2 = jnp.concatenate(
                [(proj[r, c0 + p * 2 * ATT_HDIM:c0 + (p + 1) * 2 * ATT_HDIM] * q_scale).astype(BF16)
                 for p in range(group // 2)], axis=0)
            kcat = jnp.concatenate([prev_k[2 * g], cur_k[2 * g], prev_k[2 * g + 1], cur_k[2 * g + 1]], axis=0)
            s_all = _dot_nt(q2, kcat)
            o_g = None
            for half in range(2):
                s_prev = s_all[:, (2 * half) * ATT_BLOCK:(2 * half + 1) * ATT_BLOCK]
                s_cur = s_all[:, (2 * half + 1) * ATT_BLOCK:(2 * half + 2) * ATT_BLOCK]
                if no_prev is not None:
                    s_prev = s_prev + no_prev
                s = jnp.where(upper, s_prev, s_cur)
                head0 = g * group + half
                sink = jnp.where(first_pair, sink_ref[head0], sink_ref[head0 + 2]) * LOG2E
                m = jnp.maximum(jnp.max(s, axis=-1, keepdims=True), sink)
                pexp = jnp.exp2(s - m)
                denom = jnp.sum(pexp, axis=-1, keepdims=True) + jnp.exp2(sink - m)
                pcat = jnp.concatenate([jnp.where(upper, pexp, 0.0).astype(BF16),
                                        jnp.where(upper, 0.0, pexp).astype(BF16)], axis=1)
                vcat = jnp.concatenate([prev_v[2 * g + half], cur_v[2 * g + half]], axis=0)
                o_h = _dot(pcat, vcat) * (1.0 / denom)
                o_g = o_h if o_g is None else o_g + o_h
            for p in range(group // 2):
                pair_out.append(o_g[p * ATT_BLOCK:(p + 1) * ATT_BLOCK])
        return jnp.concatenate(pair_out, axis=1), (cur_k, cur_v)

    assert HG_CHUNK == ATT_BLOCK
    st = st_ref[...]
    prev = None
    oa_chunks, o_c_blocks = [], []
    for j in range(tile // ATT_BLOCK):
        o_chunk, st = hg_chunk(j, st)
        oa_chunks.append(o_chunk)
        o_block, prev = att_block(j, prev)
        o_c_blocks.append(o_block)
    oa_fast = jnp.concatenate(oa_chunks, axis=0)
    st_fast = st
    for n in range(4):
        kv_ref[n] = prev[0][n]
        kv_ref[4 + n] = prev[1][n]
    o_c = jnp.concatenate(o_c_blocks, axis=0)

    mix_bc = _dot(jnp.concatenate([o_b.astype(BF16), o_c.astype(BF16)], axis=1), wout_ref[0, HG_WIDTH:, :])

    oa_s[...] = oa_fast

    @pl.when(factorisable)
    def _():
        st_ref[...] = st_fast

    pl.when(jnp.logical_not(factorisable))(pairwise_path)

    oa = oa_s[...]
    sq_hi, sq_lo, _ = _split3(oa * oa)
    ms_h = (_dot(sq_hi, eblk) + _dot(sq_lo, eblk)) * (1.0 / HG_DV)
    o_a = oa * lax.rsqrt(ms_h + EPS) * out_gain

    mix = mix_bc + _dot(o_a.astype(BF16), wout_ref[0, 0:HG_WIDTH, :])
    msm = jnp.mean(mix * mix, axis=-1, keepdims=True)
    o_ref[0] = x + (mix * lax.rsqrt(msm + EPS)) * (gt1 * gpost_ref[...])


def _const_spec(shape):
    nd = len(shape)
    return pl.BlockSpec(shape, lambda *_, _nd=nd: (0,) * _nd, pipeline_mode=pl.Buffered(1))


def _layer_spec(shape, layer):
    nd = len(shape)
    return pl.BlockSpec((1,) + tuple(shape[1:]), lambda *_, _nd=nd, _l=layer: (_l,) + (0,) * (_nd - 1),
                        pipeline_mode=pl.Buffered(1))


def _mix_call(layer, x, mod, g_pre, g_post, w_in, lbp, hg_norm_t, pool_bd, pool_scale, sinks, w_out, eblk, cum):
    bsz, seq, d = x.shape
    tile = min(MIX_TILE, seq)
    grid = (bsz, seq // tile)
    row = lambda v: v.reshape(1, -1)
    in_specs = [
        pl.BlockSpec((1, tile, d), lambda b, s: (b, s, 0)),
        pl.BlockSpec((1, 6, d), lambda b, s: (b, 0, 0)),
        _const_spec((1, d)), _const_spec((1, d)),
        _layer_spec(w_in.shape, layer),
        _layer_spec(lbp.shape, layer),
        _const_spec((1, HG_WIDTH)),
        _const_spec(pool_bd.shape),
        _const_spec((1, POOL_WIDTH)),
        pl.BlockSpec(memory_space=pltpu.SMEM),
        _layer_spec(w_out.shape, layer),
        _const_spec(eblk.shape),
        _const_spec(cum.shape),
    ]
    scratch = [
        pltpu.VMEM((1,) + tuple(w_in.shape[1:]), BF16),
        pltpu.VMEM((1,) + tuple(w_out.shape[1:]), BF16),
        pltpu.VMEM((HG_WIDTH, HG_WIDTH), F32),
        pltpu.VMEM((8, ATT_BLOCK, ATT_KV_WIDTH), BF16),
        pltpu.VMEM((POOL_TAIL + tile, POOL_WIDTH), F32),
        pltpu.VMEM((tile, HG_WIDTH), F32),
        pltpu.VMEM((tile, HG_WIDTH), F32),
        pltpu.VMEM((tile, HG_WIDTH), F32),
        pltpu.VMEM((tile, HG_WIDTH), F32),
        pltpu.VMEM((tile, HG_WIDTH), F32),
    ]
    return pl.pallas_call(
        _mix_kernel,
        grid=grid,
        in_specs=in_specs,
        out_specs=pl.BlockSpec((1, tile, d), lambda b, s: (b, s, 0)),
        out_shape=jax.ShapeDtypeStruct(x.shape, F32),
        scratch_shapes=scratch,
        compiler_params=pltpu.CompilerParams(
            dimension_semantics=("arbitrary", "arbitrary"), vmem_limit_bytes=VMEM_LIMIT),
        name="token_mixer",
    )(x, mod, row(g_pre), row(g_post), w_in, lbp, row(hg_norm_t), pool_bd, row(pool_scale), sinks,
      w_out, eblk, cum)


def _gelu_tanh(x):
    c = 0.7978845608028654
    return (0.5 * x) * (1.0 + jnp.tanh(x * (c + (c * 0.044715) * (x * x))))


def _ffn_kernel(x_ref, mod_ref, gpre_ref, gpost_ref, wup_ref, cw_ref, cb_ref, wdown_ref,
                o_ref,
                tail_ref, ubuf_ref, act_ref):
    s_idx = pl.program_id(1)
    tile = x_ref.shape[1]
    d_model = x_ref.shape[2]

    @pl.when(s_idx == 0)
    def _():
        tail_ref[...] = jnp.zeros_like(tail_ref)

    x = x_ref[0]
    ms = jnp.mean(x * x, axis=-1, keepdims=True)
    gain = gpre_ref[...] * (1.0 + mod_ref[0, 4:5, :])
    hb = ((x * lax.rsqrt(ms + EPS)) * gain + mod_ref[0, 3:4, :]).astype(BF16)

    def up_cols(off, slot):
        ubuf = ubuf_ref.at[slot]
        u = _dot(hb, wup_ref[0, :, off:off + FFN_CHUNK])
        ubuf[0:SUBLANES, :] = tail_ref[:, off:off + FFN_CHUNK]
        ubuf[SUBLANES:SUBLANES + tile, :] = u
        tail_ref[:, off:off + FFN_CHUNK] = ubuf[pl.ds(tile, SUBLANES), :]
        return u

    def conv_cols(u, off, slot):
        ubuf = ubuf_ref.at[slot]
        w0 = cw_ref[0:1, off:off + FFN_CHUNK]
        w1 = cw_ref[1:2, off:off + FFN_CHUNK]
        w2 = cw_ref[2:3, off:off + FFN_CHUNK]
        return (w0 * ubuf[pl.ds(SUBLANES - 2, tile), :] + w1 * ubuf[pl.ds(SUBLANES - 1, tile), :]
                + w2 * u + cb_ref[0:1, off:off + FFN_CHUNK])

    n_chunks = D_FF // FFN_CHUNK
    pending = (up_cols(0, 0), up_cols(D_FF, 1))
    for c in range(n_chunks):
        u_gate, u_val = pending
        if c + 1 < n_chunks:
            nxt = 2 * ((c + 1) % 2)
            pending = (up_cols((c + 1) * FFN_CHUNK, nxt), up_cols(D_FF + (c + 1) * FFN_CHUNK, nxt + 1))
        if c == n_chunks - FFN_EARLY_DOWN:
            k_early = c * FFN_CHUNK
            ys_early = [_dot(act_ref[:, 0:k_early], wdown_ref[0, 0:k_early, n:n + FFN_CHUNK])
                        for n in range(0, d_model, FFN_CHUNK)]
        slot = 2 * (c % 2)
        g = conv_cols(u_gate, c * FFN_CHUNK, slot)
        v = conv_cols(u_val, D_FF + c * FFN_CHUNK, slot + 1)
        act_ref[:, c * FFN_CHUNK:(c + 1) * FFN_CHUNK] = (_gelu_tanh(g) * v).astype(BF16)

    ys = [y_early + _dot(act_ref[:, k_early:], wdown_ref[0, k_early:, n:n + FFN_CHUNK])
          for y_early, n in zip(ys_early, range(0, d_model, FFN_CHUNK))]
    ssq = None
    for y in ys:
        part = jnp.sum(y * y, axis=-1, keepdims=True)
        ssq = part if ssq is None else ssq + part
    rs = lax.rsqrt(ssq * (1.0 / d_model) + EPS)
    out_gain = mod_ref[0, 5:6, :] * gpost_ref[...]
    for g, y in enumerate(ys):
        cols = slice(g * FFN_CHUNK, (g + 1) * FFN_CHUNK)
        o_ref[0, :, cols] = x[:, cols] + (y * rs) * out_gain[:, cols]


def _ffn_call(layer, x, mod, g_pre, g_post, w_up, conv_w, conv_b, w_down):
    bsz, seq, d = x.shape
    tile = min(FFN_TILE, seq)
    grid = (bsz, seq // tile)
    row = lambda v: v.reshape(1, -1)
    in_specs = [
        pl.BlockSpec((1, tile, d), lambda b, s: (b, s, 0)),
        pl.BlockSpec((1, 6, d), lambda b, s: (b, 0, 0)),
        _const_spec((1, d)), _const_spec((1, d)),
        _layer_spec(w_up.shape, layer),
        _const_spec(conv_w.shape),
        _const_spec((1, 2 * D_FF)),
        _layer_spec(w_down.shape, layer),
    ]
    scratch = [
        pltpu.VMEM((SUBLANES, 2 * D_FF), F32),
        pltpu.VMEM((4, SUBLANES + tile, FFN_CHUNK), F32),
        pltpu.VMEM((tile, D_FF), BF16),
    ]
    return pl.pallas_call(
        _ffn_kernel,
        grid=grid,
        in_specs=in_specs,
        out_specs=pl.BlockSpec((1, tile, d), lambda b, s: (b, s, 0)),
        out_shape=jax.ShapeDtypeStruct(x.shape, F32),
        scratch_shapes=scratch,
        compiler_params=pltpu.CompilerParams(
            dimension_semantics=("arbitrary", "arbitrary"), vmem_limit_bytes=VMEM_LIMIT),
        name="conv_ffn",
    )(x, mod, row(g_pre), row(g_post), w_up, conv_w, row(conv_b), w_down)


def _block_diag(blocks):
    g, c, _ = blocks.shape
    eye = jnp.eye(g, dtype=blocks.dtype)
    return (eye[:, None, :, None] * blocks[:, :, None, :]).reshape(g * c, g * c)


def _cum_matrix(tile):
    r = jnp.arange(tile)
    same = (r[:, None] // HG_CHUNK) == (r[None, :] // HG_CHUNK)
    return (same & (r[None, :] <= r[:, None])).astype(BF16)


def kernel(x, c, w_ada, b_ada, g_pre_mix, g_post_mix, w_in, hg_lb_logits, hg_norm, pool_w, pool_scale,
           attn_sinks, w_out, g_pre_ffn, g_post_ffn, w_up, conv_w, conv_b, w_down):
    depth = w_ada.shape[0]
    bsz, seq, d = x.shape
    mod_all, lbp_all = _ada_call(c, w_ada, b_ada, hg_lb_logits)
    mod_all = mod_all.reshape(depth, bsz, 6, d)
    eblk = _block_diag(jnp.ones((HG_HEADS, HG_DK, HG_DV), BF16))
    cum = _cum_matrix(min(MIX_TILE, seq))
    w_up_b, w_down_b = w_up.astype(BF16), w_down.astype(BF16)
    for l in range(depth):
        x = _mix_call(
            l, x, mod_all[l], g_pre_mix[l], g_post_mix[l], w_in, lbp_all,
            jnp.tile(hg_norm[l], HG_HEADS), _block_diag(pool_w[l]).astype(BF16), pool_scale[l],
            attn_sinks[l], w_out, eblk, cum)
        x = _ffn_call(
            l, x, mod_all[l], g_pre_ffn[l], g_post_ffn[l], w_up_b, conv_w[l], conv_b[l], w_down_b)
    return x
```

```python
import jax
import jax.numpy as jnp
from jax import lax
from jax.experimental import pallas as pl
from jax.experimental.pallas import tpu as pltpu

F32 = jnp.float32
BF16 = jnp.bfloat16

D_MODEL = 1024
HG_HEADS = 4
HG_DK = 64
HG_DV = 64
HG_WIDTH = HG_HEADS * HG_DV
POOL_WINDOWS = (2, 4, 8, 16)
POOL_GDIM = 64
POOL_WIDTH = len(POOL_WINDOWS) * POOL_GDIM
POOL_TAIL = 32
ATT_HEADS = 8
ATT_KV_HEADS = 2
ATT_HDIM = 64
ATT_WIDTH = ATT_HEADS * ATT_HDIM
ATT_KV_WIDTH = ATT_KV_HEADS * ATT_HDIM
WINDOW = 128
ATT_BLOCK = WINDOW
D_FF = 2816
CONV_WIDTH = 3
EPS = 1e-6
LOG2E = 1.4426950408889634
D_IN = 4 * HG_WIDTH + POOL_WIDTH + ATT_WIDTH + 2 * ATT_KV_WIDTH
OFF_HQ = 0
OFF_HF = OFF_HQ + HG_WIDTH
OFF_HI = OFF_HF + HG_WIDTH
OFF_HG = OFF_HI + HG_WIDTH
OFF_PV = OFF_HG + HG_WIDTH
OFF_AQ = OFF_PV + POOL_WIDTH
OFF_AK = OFF_AQ + ATT_WIDTH
OFF_AV = OFF_AK + ATT_KV_WIDTH

HG_CHUNK = 128
HG_SUB = 32
HG_BLOCK = 16
HG_SAFE_EXP = 60.0
MIX_TILE = 512
PROJ_GROUP = 512
FFN_TILE = 512
FFN_CHUNK = 256
FFN_EARLY_DOWN = 2
ADA_TILE = 1024
SUBLANES = 8
VMEM_LIMIT = 56 * 1024 * 1024


def _sigmoid(x):
    return 1.0 / (1.0 + jnp.exp(-x))


def _split3(x):
    hi = x.astype(BF16)
    r1 = x - hi.astype(F32)
    mid = r1.astype(BF16)
    lo = (r1 - mid.astype(F32)).astype(BF16)
    return hi, mid, lo


def _dot(a, b):
    return jnp.dot(a, b, preferred_element_type=F32)


def _dot_nt(a, b):
    return lax.dot_general(a, b, (((1,), (1,)), ((), ())), preferred_element_type=F32)


def _dot_tn(a, b):
    return lax.dot_general(a, b, (((0,), (0,)), ((), ())), preferred_element_type=F32)


def _ada_kernel(c_ref, w_ref, b_ref, lbl_ref, mod_ref, lbp_ref):
    c = c_ref[...]
    ca = c * _sigmoid(c)
    acc = None
    wb = w_ref[0].astype(BF16)
    for piece in _split3(ca):
        t = _dot(piece, wb)
        acc = t if acc is None else acc + t
    mod_ref[0] = acc + b_ref[0]

    logits = lbl_ref[...]
    depth = logits.shape[0]
    mx = jnp.max(logits, axis=0, keepdims=True)
    ex = jnp.exp(logits - mx)
    sm = ex / jnp.sum(ex, axis=0, keepdims=True)
    run = sm[0:1]
    first = run
    for l in range(depth):
        if l > 0:
            run = run + sm[l:l + 1]
        lb = run - first
        lbp_ref[l, 0:1, :] = jnp.log(lb)
        lbp_ref[l, 1:2, :] = jnp.log1p(-lb)
        lbp_ref[l, 2:3, :] = 1.0 - lb


def _ada_call(c, w_ada, b_ada, hg_lb_logits):
    depth, d, n = w_ada.shape
    bsz = c.shape[0]
    hk = hg_lb_logits.shape[1]
    grid = (depth, n // ADA_TILE)
    return pl.pallas_call(
        _ada_kernel,
        grid=grid,
        in_specs=[
            pl.BlockSpec((bsz, d), lambda l, j: (0, 0)),
            pl.BlockSpec((1, d, ADA_TILE), lambda l, j: (l, 0, j)),
            pl.BlockSpec((1, 1, ADA_TILE), lambda l, j: (l, 0, j)),
            pl.BlockSpec((depth, hk), lambda l, j: (0, 0)),
        ],
        out_specs=[
            pl.BlockSpec((1, bsz, ADA_TILE), lambda l, j: (l, 0, j)),
            pl.BlockSpec((depth, 3, hk), lambda l, j: (0, 0, 0)),
        ],
        out_shape=[
            jax.ShapeDtypeStruct((depth, bsz, n), F32),
            jax.ShapeDtypeStruct((depth, 3, hk), F32),
        ],
        compiler_params=pltpu.CompilerParams(
            dimension_semantics=("arbitrary", "arbitrary"), vmem_limit_bytes=VMEM_LIMIT),
        name="adaln_params",
    )(c, w_ada, b_ada.reshape(depth, 1, n), hg_lb_logits)


def _kv_variants(t, low_half):
    sw = pltpu.roll(t, ATT_HDIM, 1)
    return (jnp.where(low_half, t, 0.0).astype(BF16), jnp.where(low_half, 0.0, sw).astype(BF16),
            jnp.where(low_half, sw, 0.0).astype(BF16), jnp.where(low_half, 0.0, t).astype(BF16))


class _Columns:
    def __init__(self, group_fn, width):
        self.group_fn, self.width, self.pieces = group_fn, width, {}

    def __getitem__(self, idx):
        rows, cols = idx
        piece, lo = divmod(cols.start, self.width)
        assert cols.stop - cols.start <= self.width - lo, "column slice straddles two groups"
        if piece not in self.pieces:
            self.pieces[piece] = self.group_fn(piece * self.width)
        return self.pieces[piece][rows, lo:lo + cols.stop - cols.start]


def _mix_kernel(x_ref, mod_ref, gpre_ref, gpost_ref, win32_ref, lbp_ref, hgn_ref, poolw_ref,
                pools_ref, sink_ref, wout32_ref, eblk_ref, cum_ref,
                o_ref,
                win_ref, wout_ref, st_ref, kv_ref, ptail_ref, qf_s, b_s, kk_s, vi_s, oa_s):
    s_idx = pl.program_id(1)
    tile = x_ref.shape[1]

    @pl.when((pl.program_id(0) == 0) & (s_idx == 0))
    def _():
        win_ref[...] = win32_ref[...].astype(BF16)
        wout_ref[...] = wout32_ref[...].astype(BF16)

    @pl.when(s_idx == 0)
    def _():
        st_ref[...] = jnp.zeros_like(st_ref)
        kv_ref[...] = jnp.zeros_like(kv_ref)
        ptail_ref[0:POOL_TAIL, :] = jnp.zeros((POOL_TAIL, POOL_WIDTH), F32)

    x = x_ref[0]
    gt1 = mod_ref[0, 2:3, :]
    ms = jnp.mean(x * x, axis=-1, keepdims=True)
    hb = ((x * lax.rsqrt(ms + EPS)) * (gpre_ref[...] * (1.0 + mod_ref[0, 1:2, :])) + mod_ref[0, 0:1, :]).astype(BF16)
    proj = _Columns(lambda c: _dot(hb, win_ref[0, :, c:c + PROJ_GROUP]), PROJ_GROUP)

    eblk = eblk_ref[...]
    emask = eblk > 0

    q = proj[:, OFF_HQ:OFF_HQ + HG_WIDTH]
    z = proj[:, OFF_HF:OFF_HF + HG_WIDTH]
    qf = q * _sigmoid(q) * (HG_DK ** -0.5)
    log_sig = jnp.minimum(z, 0.0) - jnp.log(1.0 + jnp.exp(-jnp.abs(z)))
    la = lbp_ref[0, 0:1, :]
    lc = lbp_ref[0, 1:2, :] + log_sig
    log_f = jnp.maximum(la, lc) + jnp.log(1.0 + jnp.exp(-jnp.abs(la - lc)))
    kk = lbp_ref[0, 2:3, :] * _sigmoid(-z)
    vi = proj[:, OFF_HI:OFF_HI + HG_WIDTH]
    gate = proj[:, OFF_HG:OFF_HG + HG_WIDTH]
    out_gain = hgn_ref[...] * (gate * _sigmoid(gate))
    parts = jnp.concatenate(_split3(log_f), axis=1)
    cums = jnp.concatenate(
        [_dot(cum_ref[...], parts[c:c + HG_CHUNK]) for c in range(0, tile, HG_CHUNK)], axis=0)
    w = HG_WIDTH
    b_loc = cums[:, 0:w] + cums[:, w:2 * w] + cums[:, 2 * w:3 * w]

    sub = HG_SUB
    spread = None
    for i in range(tile // sub):
        bl = b_loc[i * sub:(i + 1) * sub]
        sp = jnp.max(jnp.abs(bl - bl[sub // 2 - 1:sub // 2]))
        spread = sp if spread is None else jnp.maximum(spread, sp)
    factorisable = spread <= HG_SAFE_EXP

    def pairwise_path():
        qf_s[...] = qf
        b_s[...] = b_loc
        kk_s[...] = kk
        vi_s[...] = vi
        rows16 = lax.broadcasted_iota(jnp.int32, (HG_BLOCK, HG_WIDTH), 0)
        per_chunk = HG_CHUNK // HG_BLOCK

        def hg_step(i, carry):
            r0 = pl.multiple_of(i * HG_BLOCK, HG_BLOCK)
            qb = qf_s[pl.ds(r0, HG_BLOCK), :]
            bb = b_s[pl.ds(r0, HG_BLOCK), :]
            kb = kk_s[pl.ds(r0, HG_BLOCK), :]
            b_prev = jnp.where(i % per_chunk == 0, 0.0, b_s[pl.ds(jnp.maximum(r0 - 1, 0), 1), :])
            b_end = b_s[pl.ds(r0 + HG_BLOCK - 1, 1), :]
            pieces = []
            for j in range(HG_BLOCK):
                bj = b_s[pl.ds(r0 + j, 1), :]
                kj_row = kk_s[pl.ds(r0 + j, 1), :]
                d = jnp.where(rows16 >= j, (qb * kj_row) * jnp.exp(bb - bj), 0.0)
                pieces.append(d.astype(BF16))
            dall = jnp.concatenate(pieces, axis=0)
            sall = _dot(dall, eblk)
            acc = jnp.zeros((HG_BLOCK, HG_WIDTH), F32)
            for j in range(HG_BLOCK):
                vj = vi_s[pl.ds(r0 + j, 1), :]
                acc = acc + sall[j * HG_BLOCK:(j + 1) * HG_BLOCK, :] * vj
            st_old = st_ref[...]
            qs_b = (qb * jnp.exp(bb - b_prev)).astype(BF16)
            ke_b = (kb * jnp.exp(b_end - bb)).astype(BF16)
            oa_s[pl.ds(r0, HG_BLOCK), :] = acc + _dot_nt(qs_b, st_old.astype(BF16))
            vb_b = vi_s[pl.ds(r0, HG_BLOCK), :].astype(BF16)
            st_ref[...] = st_old * jnp.exp(b_end - b_prev) + jnp.where(emask, _dot_tn(vb_b, ke_b), 0.0)
            return carry

        lax.fori_loop(0, tile // HG_BLOCK, hg_step, 0)

    b2 = b_loc * LOG2E
    n_sub = HG_CHUNK // sub
    half = HG_CHUNK // 2
    lane_head = lax.broadcasted_iota(jnp.int32, (HG_CHUNK, HG_WIDTH), 1) // HG_DK
    head_keep = [jnp.where(lane_head == hd, 1.0, 0.0).astype(BF16) for hd in range(HG_HEADS)]

    def stack_heads(a):
        ab = a.astype(BF16)
        return jnp.concatenate([ab * keep for keep in head_keep], axis=0)

    ti = lax.broadcasted_iota(jnp.int32, (HG_HEADS * HG_CHUNK, HG_CHUNK), 0) % HG_CHUNK
    tj = lax.broadcasted_iota(jnp.int32, (HG_HEADS * HG_CHUNK, HG_CHUNK), 1)
    m_same32 = (ti // sub == tj // sub) & (ti >= tj)
    m_same64 = (ti // (2 * sub) == tj // (2 * sub)) & (ti // sub > tj // sub)
    zsub = jnp.zeros((sub, HG_WIDTH), F32)
    zhalf = jnp.zeros((half, HG_WIDTH), F32)
    def hg_chunk(c, st):
        r = slice(c * HG_CHUNK, (c + 1) * HG_CHUNK)
        bl = b2[r]
        qc = qf[r]
        kc = kk[r]
        vc = vi[r]
        blk = lambda a, i: a[i * sub:(i + 1) * sub]
        row = lambda i: bl[i:i + 1]
        mids = [row(i * sub + sub // 2 - 1) for i in range(n_sub)]
        q_a = jnp.concatenate([blk(qc, i) * jnp.exp2(blk(bl, i) - mids[i]) for i in range(n_sub)], axis=0)
        k_a = jnp.concatenate([blk(kc, i) * jnp.exp2(mids[i] - blk(bl, i)) for i in range(n_sub)], axis=0)
        q_b = jnp.concatenate(
            [zsub if i % 2 == 0 else blk(qc, i) * jnp.exp2(blk(bl, i) - row(i * sub - 1)) for i in range(n_sub)],
            axis=0)
        k_b = jnp.concatenate(
            [blk(kc, i) * jnp.exp2(row((i + 1) * sub - 1) - blk(bl, i)) if i % 2 == 0 else zsub
             for i in range(n_sub)], axis=0)
        q_c = jnp.concatenate([zhalf, qc[half:] * jnp.exp2(bl[half:] - row(half - 1))], axis=0)
        k_c = jnp.concatenate([kc[:half] * jnp.exp2(row(half - 1) - bl[:half]), zhalf], axis=0)
        s_a = _dot_nt(stack_heads(q_a), k_a.astype(BF16))
        s_b = _dot_nt(stack_heads(q_b), k_b.astype(BF16))
        s_c = _dot_nt(stack_heads(q_c), k_c.astype(BF16))
        p = jnp.where(m_same32, s_a, jnp.where(m_same64, s_b, s_c)).astype(BF16)
        pcat = jnp.concatenate([p[hd * HG_CHUNK:(hd + 1) * HG_CHUNK] for hd in range(HG_HEADS)], axis=1)
        o_intra = _dot(pcat, stack_heads(vc))
        b_end = row(HG_CHUNK - 1)
        qs = (qc * jnp.exp2(bl)).astype(BF16)
        ke = (kc * jnp.exp2(b_end - bl)).astype(BF16)
        o_chunk = o_intra + _dot_nt(qs, st.astype(BF16))
        return o_chunk, st * jnp.exp2(b_end) + jnp.where(emask, _dot_tn(vc.astype(BF16), ke), 0.0)

    pv = proj[:, OFF_PV:OFF_PV + POOL_WIDTH]
    ptail_ref[POOL_TAIL:POOL_TAIL + tile, :] = pv
    lane_p = lax.broadcasted_iota(jnp.int32, (1, POOL_WIDTH), 1)
    win_l = jnp.left_shift(2, lane_p // POOL_GDIM)
    groups_per_col = 128 // POOL_GDIM
    first_group = lax.broadcasted_iota(jnp.int32, (tile, 128), 1) < POOL_GDIM
    cols = []
    for col in range(POOL_WIDTH // 128):
        cur = ptail_ref[:, col * 128:(col + 1) * 128]
        start = 0
        sums = []
        for level in range((col + 1) * groups_per_col):
            shift = 1 << level
            rows = cur.shape[0]
            cur = cur[SUBLANES:] + cur[SUBLANES - shift:rows - shift]
            start += SUBLANES
            sums.append(cur[POOL_TAIL - start:])
        cols.append(jnp.where(first_group, sums[-2], sums[-1]))
    acc_p = jnp.concatenate(cols, axis=1)
    t_glob = s_idx * tile + lax.broadcasted_iota(jnp.int32, (tile, POOL_WIDTH), 0)
    cnt = jnp.minimum(t_glob + 1, win_l).astype(F32)
    pooled = acc_p / cnt - pv
    ptail_ref[0:POOL_TAIL, :] = ptail_ref[pl.ds(tile, POOL_TAIL), :]
    o_b = _dot(pooled.astype(BF16), poolw_ref[...]) * pools_ref[...]

    group = ATT_HEADS // ATT_KV_HEADS
    pair_rows = (group // 2) * ATT_BLOCK
    low_half = lax.broadcasted_iota(jnp.int32, (ATT_BLOCK, ATT_KV_WIDTH), 1) < ATT_HDIM
    qi = lax.broadcasted_iota(jnp.int32, (pair_rows, ATT_BLOCK), 0) % ATT_BLOCK
    kj = lax.broadcasted_iota(jnp.int32, (pair_rows, ATT_BLOCK), 1)
    upper = kj > qi
    first_pair = lax.broadcasted_iota(jnp.int32, (pair_rows, 1), 0) < ATT_BLOCK
    q_scale = (ATT_HDIM ** -0.5) * LOG2E
    def att_block(a, prev):
        r = slice(a * ATT_BLOCK, (a + 1) * ATT_BLOCK)
        cur_k = _kv_variants(proj[r, OFF_AK:OFF_AK + ATT_KV_WIDTH], low_half)
        cur_v = _kv_variants(proj[r, OFF_AV:OFF_AV + ATT_KV_WIDTH], low_half)
        if prev is None:
            prev_k = tuple(kv_ref[n] for n in range(4))
            prev_v = tuple(kv_ref[4 + n] for n in range(4))
            no_prev = jnp.where(upper & (s_idx == 0), -jnp.inf, 0.0)
        else:
            prev_k, prev_v = prev
            no_prev = None
        pair_out = []
        for g in range(ATT_KV_HEADS):
            c0 = OFF_AQ + g * group * ATT_HDIM
            q2 = jnp.concatenate(
                [(proj[r, c0 + p * 2 * ATT_HDIM:c0 + (p + 1) * 2 * ATT_HDIM] * q_scale).astype(BF16)
                 for p in range(group // 2)], axis=0)
            kcat = jnp.concatenate([prev_k[2 * g], cur_k[2 * g], prev_k[2 * g + 1], cur_k[2 * g + 1]], axis=0)
            s_all = _dot_nt(q2, kcat)
            o_g = None
            for half in range(2):
                s_prev = s_all[:, (2 * half) * ATT_BLOCK:(2 * half + 1) * ATT_BLOCK]
                s_cur = s_all[:, (2 * half + 1) * ATT_BLOCK:(2 * half + 2) * ATT_BLOCK]
                if no_prev is not None:
                    s_prev = s_prev + no_prev
                s = jnp.where(upper, s_prev, s_cur)
                head0 = g * group + half
                sink = jnp.where(first_pair, sink_ref[head0], sink_ref[head0 + 2]) * LOG2E
                m = jnp.maximum(jnp.max(s, axis=-1, keepdims=True), sink)
                pexp = jnp.exp2(s - m)
                denom = jnp.sum(pexp, axis=-1, keepdims=True) + jnp.exp2(sink - m)
                pcat = jnp.concatenate([jnp.where(upper, pexp, 0.0).astype(BF16),
                                        jnp.where(upper, 0.0, pexp).astype(BF16)], axis=1)
                vcat = jnp.concatenate([prev_v[2 * g + half], cur_v[2 * g + half]], axis=0)
                o_h = _dot(pcat, vcat) * (1.0 / denom)
                o_g = o_h if o_g is None else o_g + o_h
            for p in range(group // 2):
                pair_out.append(o_g[p * ATT_BLOCK:(p + 1) * ATT_BLOCK])
        return jnp.concatenate(pair_out, axis=1), (cur_k, cur_v)

    assert HG_CHUNK == ATT_BLOCK
    st = st_ref[...]
    prev = None
    oa_chunks, o_c_blocks = [], []
    for j in range(tile // ATT_BLOCK):
        o_chunk, st = hg_chunk(j, st)
        oa_chunks.append(o_chunk)
        o_block, prev = att_block(j, prev)
        o_c_blocks.append(o_block)
    oa_fast = jnp.concatenate(oa_chunks, axis=0)
    st_fast = st
    for n in range(4):
        kv_ref[n] = prev[0][n]
        kv_ref[4 + n] = prev[1][n]
    o_c = jnp.concatenate(o_c_blocks, axis=0)

    mix_bc = _dot(jnp.concatenate([o_b.astype(BF16), o_c.astype(BF16)], axis=1), wout_ref[0, HG_WIDTH:, :])

    oa_s[...] = oa_fast

    @pl.when(factorisable)
    def _():
        st_ref[...] = st_fast

    pl.when(jnp.logical_not(factorisable))(pairwise_path)

    oa = oa_s[...]
    sq_hi, sq_lo, _ = _split3(oa * oa)
    ms_h = (_dot(sq_hi, eblk) + _dot(sq_lo, eblk)) * (1.0 / HG_DV)
    o_a = oa * lax.rsqrt(ms_h + EPS) * out_gain

    mix = mix_bc + _dot(o_a.astype(BF16), wout_ref[0, 0:HG_WIDTH, :])
    msm = jnp.mean(mix * mix, axis=-1, keepdims=True)
    o_ref[0] = x_ref[0] + (mix * lax.rsqrt(msm + EPS)) * (gt1 * gpost_ref[...])


def _const_spec(shape):
    nd = len(shape)
    return pl.BlockSpec(shape, lambda *_, _nd=nd: (0,) * _nd, pipeline_mode=pl.Buffered(1))


def _layer_spec(shape, layer):
    nd = len(shape)
    return pl.BlockSpec((1,) + tuple(shape[1:]), lambda *_, _nd=nd, _l=layer: (_l,) + (0,) * (_nd - 1),
                        pipeline_mode=pl.Buffered(1))


def _mix_call(layer, x, mod, g_pre, g_post, w_in, lbp, hg_norm_t, pool_bd, pool_scale, sinks, w_out, eblk, cum):
    bsz, seq, d = x.shape
    tile = min(MIX_TILE, seq)
    grid = (bsz, seq // tile)
    row = lambda v: v.reshape(1, -1)
    in_specs = [
        pl.BlockSpec((1, tile, d), lambda b, s: (b, s, 0)),
        pl.BlockSpec((1, 6, d), lambda b, s: (b, 0, 0)),
        _const_spec((1, d)), _const_spec((1, d)),
        _layer_spec(w_in.shape, layer),
        _layer_spec(lbp.shape, layer),
        _const_spec((1, HG_WIDTH)),
        _const_spec(pool_bd.shape),
        _const_spec((1, POOL_WIDTH)),
        pl.BlockSpec(memory_space=pltpu.SMEM),
        _layer_spec(w_out.shape, layer),
        _const_spec(eblk.shape),
        _const_spec(cum.shape),
    ]
    scratch = [
        pltpu.VMEM((1,) + tuple(w_in.shape[1:]), BF16),
        pltpu.VMEM((1,) + tuple(w_out.shape[1:]), BF16),
        pltpu.VMEM((HG_WIDTH, HG_WIDTH), F32),
        pltpu.VMEM((8, ATT_BLOCK, ATT_KV_WIDTH), BF16),
        pltpu.VMEM((POOL_TAIL + tile, POOL_WIDTH), F32),
        pltpu.VMEM((tile, HG_WIDTH), F32),
        pltpu.VMEM((tile, HG_WIDTH), F32),
        pltpu.VMEM((tile, HG_WIDTH), F32),
        pltpu.VMEM((tile, HG_WIDTH), F32),
        pltpu.VMEM((tile, HG_WIDTH), F32),
    ]
    return pl.pallas_call(
        _mix_kernel,
        grid=grid,
        in_specs=in_specs,
        out_specs=pl.BlockSpec((1, tile, d), lambda b, s: (b, s, 0)),
        out_shape=jax.ShapeDtypeStruct(x.shape, F32),
        scratch_shapes=scratch,
        compiler_params=pltpu.CompilerParams(
            dimension_semantics=("arbitrary", "arbitrary"), vmem_limit_bytes=VMEM_LIMIT),
        name="token_mixer",
    )(x, mod, row(g_pre), row(g_post), w_in, lbp, row(hg_norm_t), pool_bd, row(pool_scale), sinks,
      w_out, eblk, cum)


def _gelu_tanh(x):
    c = 0.7978845608028654
    return (0.5 * x) * (1.0 + jnp.tanh(x * (c + (c * 0.044715) * (x * x))))


def _ffn_kernel(x_ref, mod_ref, gpre_ref, gpost_ref, wup_ref, cw_ref, cb_ref, wdown_ref,
                o_ref,
                tail_ref, act_ref):
    s_idx = pl.program_id(1)
    tile = x_ref.shape[1]
    d_model = x_ref.shape[2]

    @pl.when(s_idx == 0)
    def _():
        tail_ref[...] = jnp.zeros_like(tail_ref)

    x = x_ref[0]
    ms = jnp.mean(x * x, axis=-1, keepdims=True)
    gain = gpre_ref[...] * (1.0 + mod_ref[0, 4:5, :])
    hb = ((x * lax.rsqrt(ms + EPS)) * gain + mod_ref[0, 3:4, :]).astype(BF16)

    row8 = lax.broadcasted_iota(jnp.int32, (SUBLANES, FFN_CHUNK), 0)

    def up_cols(off):
        u = _dot(hb, wup_ref[0, :, off:off + FFN_CHUNK])
        before1 = tail_ref[SUBLANES - 1:SUBLANES, off:off + FFN_CHUNK]
        before2 = tail_ref[SUBLANES - 2:SUBLANES - 1, off:off + FFN_CHUNK]
        tail_ref[:, off:off + FFN_CHUNK] = u[tile - SUBLANES:]
        return u, before1, before2

    def conv_cols(up, off):
        u, before1, before2 = up
        s1 = pltpu.roll(u, 1, 0)
        s2 = pltpu.roll(u, 2, 0)
        s1 = jnp.concatenate([jnp.where(row8 == 0, before1, s1[0:SUBLANES]), s1[SUBLANES:]], axis=0)
        head2 = jnp.where(row8 == 0, before2, jnp.where(row8 == 1, before1, s2[0:SUBLANES]))
        s2 = jnp.concatenate([head2, s2[SUBLANES:]], axis=0)
        w0 = cw_ref[0:1, off:off + FFN_CHUNK]
        w1 = cw_ref[1:2, off:off + FFN_CHUNK]
        w2 = cw_ref[2:3, off:off + FFN_CHUNK]
        return w0 * s2 + w1 * s1 + w2 * u + cb_ref[0:1, off:off + FFN_CHUNK]

    n_chunks = D_FF // FFN_CHUNK
    pending = (up_cols(0), up_cols(D_FF))
    for c in range(n_chunks):
        u_gate, u_val = pending
        if c + 1 < n_chunks:
            pending = (up_cols((c + 1) * FFN_CHUNK), up_cols(D_FF + (c + 1) * FFN_CHUNK))
        if c == n_chunks - FFN_EARLY_DOWN:
            k_early = c * FFN_CHUNK
            ys_early = [_dot(act_ref[:, 0:k_early], wdown_ref[0, 0:k_early, n:n + FFN_CHUNK])
                        for n in range(0, d_model, FFN_CHUNK)]
        g = conv_cols(u_gate, c * FFN_CHUNK)
        v = conv_cols(u_val, D_FF + c * FFN_CHUNK)
        act_ref[:, c * FFN_CHUNK:(c + 1) * FFN_CHUNK] = (_gelu_tanh(g) * v).astype(BF16)

    ys = [y_early + _dot(act_ref[:, k_early:], wdown_ref[0, k_early:, n:n + FFN_CHUNK])
          for y_early, n in zip(ys_early, range(0, d_model, FFN_CHUNK))]
    ssq = None
    for y in ys:
        part = jnp.sum(y * y, axis=-1, keepdims=True)
        ssq = part if ssq is None else ssq + part
    rs = lax.rsqrt(ssq * (1.0 / d_model) + EPS)
    out_gain = mod_ref[0, 5:6, :] * gpost_ref[...]
    for g, y in enumerate(ys):
        cols = slice(g * FFN_CHUNK, (g + 1) * FFN_CHUNK)
        o_ref[0, :, cols] = x_ref[0, :, cols] + (y * rs) * out_gain[:, cols]


def _ffn_call(layer, x, mod, g_pre, g_post, w_up, conv_w, conv_b, w_down):
    bsz, seq, d = x.shape
    tile = min(FFN_TILE, seq)
    grid = (bsz, seq // tile)
    row = lambda v: v.reshape(1, -1)
    in_specs = [
        pl.BlockSpec((1, tile, d), lambda b, s: (b, s, 0)),
        pl.BlockSpec((1, 6, d), lambda b, s: (b, 0, 0)),
        _const_spec((1, d)), _const_spec((1, d)),
        _layer_spec(w_up.shape, layer),
        _const_spec(conv_w.shape),
        _const_spec((1, 2 * D_FF)),
        _layer_spec(w_down.shape, layer),
    ]
    scratch = [
        pltpu.VMEM((SUBLANES, 2 * D_FF), F32),
        pltpu.VMEM((tile, D_FF), BF16),
    ]
    return pl.pallas_call(
        _ffn_kernel,
        grid=grid,
        in_specs=in_specs,
        out_specs=pl.BlockSpec((1, tile, d), lambda b, s: (b, s, 0)),
        out_shape=jax.ShapeDtypeStruct(x.shape, F32),
        scratch_shapes=scratch,
        compiler_params=pltpu.CompilerParams(
            dimension_semantics=("arbitrary", "arbitrary"), vmem_limit_bytes=VMEM_LIMIT),
        name="conv_ffn",
    )(x, mod, row(g_pre), row(g_post), w_up, conv_w, row(conv_b), w_down)


def _block_diag(blocks):
    g, c, _ = blocks.shape
    eye = jnp.eye(g, dtype=blocks.dtype)
    return (eye[:, None, :, None] * blocks[:, :, None, :]).reshape(g * c, g * c)


def _cum_matrix(rows):
    r = jnp.arange(rows)
    return (r[None, :] <= r[:, None]).astype(BF16)


def kernel(x, c, w_ada, b_ada, g_pre_mix, g_post_mix, w_in, hg_lb_logits, hg_norm, pool_w, pool_scale,
           attn_sinks, w_out, g_pre_ffn, g_post_ffn, w_up, conv_w, conv_b, w_down):
    depth = w_ada.shape[0]
    bsz, seq, d = x.shape
    mod_all, lbp_all = _ada_call(c, w_ada, b_ada, hg_lb_logits)
    mod_all = mod_all.reshape(depth, bsz, 6, d)
    eblk = _block_diag(jnp.ones((HG_HEADS, HG_DK, HG_DV), BF16))
    cum = _cum_matrix(HG_CHUNK)
    w_up_b, w_down_b = w_up.astype(BF16), w_down.astype(BF16)
    for l in range(depth):
        x = _mix_call(
            l, x, mod_all[l], g_pre_mix[l], g_post_mix[l], w_in, lbp_all,
            jnp.tile(hg_norm[l], HG_HEADS), _block_diag(pool_w[l]).astype(BF16), pool_scale[l],
            attn_sinks[l], w_out, eblk, cum)
        x = _ffn_call(
            l, x, mod_all[l], g_pre_ffn[l], g_post_ffn[l], w_up_b, conv_w[l], conv_b[l], w_down_b)
    return x
```

```python
import jax
import jax.numpy as jnp
from jax import lax
from jax.experimental import pallas as pl
from jax.experimental.pallas import tpu as pltpu

F32 = jnp.float32
BF16 = jnp.bfloat16

D_MODEL = 1024
HG_HEADS = 4
HG_DK = 64
HG_DV = 64
HG_WIDTH = HG_HEADS * HG_DV
POOL_WINDOWS = (2, 4, 8, 16)
POOL_GDIM = 64
POOL_WIDTH = len(POOL_WINDOWS) * POOL_GDIM
POOL_TAIL = 32
ATT_HEADS = 8
ATT_KV_HEADS = 2
ATT_HDIM = 64
ATT_WIDTH = ATT_HEADS * ATT_HDIM
ATT_KV_WIDTH = ATT_KV_HEADS * ATT_HDIM
WINDOW = 128
ATT_BLOCK = WINDOW
D_FF = 2816
CONV_WIDTH = 3
EPS = 1e-6
LOG2E = 1.4426950408889634
D_IN = 4 * HG_WIDTH + POOL_WIDTH + ATT_WIDTH + 2 * ATT_KV_WIDTH
OFF_HQ = 0
OFF_HF = OFF_HQ + HG_WIDTH
OFF_HI = OFF_HF + HG_WIDTH
OFF_HG = OFF_HI + HG_WIDTH
OFF_PV = OFF_HG + HG_WIDTH
OFF_AQ = OFF_PV + POOL_WIDTH
OFF_AK = OFF_AQ + ATT_WIDTH
OFF_AV = OFF_AK + ATT_KV_WIDTH

HG_CHUNK = 128
HG_SUB = 32
HG_BLOCK = 16
HG_SAFE_EXP = 60.0
MIX_TILE = 512
PROJ_GROUP = 1024
FFN_TILE = 512
FFN_CHUNK = 256
FFN_EARLY_DOWN = 2
ADA_TILE = 3072
SUBLANES = 8
VMEM_LIMIT = 56 * 1024 * 1024


def _sigmoid(x):
    return 1.0 / (1.0 + jnp.exp(-x))


def _split3(x):
    hi = x.astype(BF16)
    r1 = x - hi.astype(F32)
    mid = r1.astype(BF16)
    lo = (r1 - mid.astype(F32)).astype(BF16)
    return hi, mid, lo


def _dot(a, b):
    return jnp.dot(a, b, preferred_element_type=F32)


def _dot_nt(a, b):
    return lax.dot_general(a, b, (((1,), (1,)), ((), ())), preferred_element_type=F32)


def _dot_tn(a, b):
    return lax.dot_general(a, b, (((0,), (0,)), ((), ())), preferred_element_type=F32)


def _ada_kernel(c_ref, w_ref, b_ref, lbl_ref, mod_ref, lbp_ref):
    c = c_ref[...]
    ca = c * _sigmoid(c)
    acc = None
    wb = w_ref[0].astype(BF16)
    for piece in _split3(ca):
        t = _dot(piece, wb)
        acc = t if acc is None else acc + t
    mod_ref[0] = acc + b_ref[0]

    logits = lbl_ref[...]
    depth = logits.shape[0]
    mx = jnp.max(logits, axis=0, keepdims=True)
    ex = jnp.exp(logits - mx)
    sm = ex / jnp.sum(ex, axis=0, keepdims=True)
    run = sm[0:1]
    first = run
    for l in range(depth):
        if l > 0:
            run = run + sm[l:l + 1]
        lb = run - first
        lbp_ref[l, 0:1, :] = jnp.log(lb)
        lbp_ref[l, 1:2, :] = jnp.log1p(-lb)
        lbp_ref[l, 2:3, :] = 1.0 - lb


def _ada_call(c, w_ada, b_ada, hg_lb_logits):
    depth, d, n = w_ada.shape
    bsz = c.shape[0]
    hk = hg_lb_logits.shape[1]
    grid = (depth, n // ADA_TILE)
    return pl.pallas_call(
        _ada_kernel,
        grid=grid,
        in_specs=[
            pl.BlockSpec((bsz, d), lambda l, j: (0, 0)),
            pl.BlockSpec((1, d, ADA_TILE), lambda l, j: (l, 0, j)),
            pl.BlockSpec((1, 1, ADA_TILE), lambda l, j: (l, 0, j)),
            pl.BlockSpec((depth, hk), lambda l, j: (0, 0)),
        ],
        out_specs=[
            pl.BlockSpec((1, bsz, ADA_TILE), lambda l, j: (l, 0, j)),
            pl.BlockSpec((depth, 3, hk), lambda l, j: (0, 0, 0)),
        ],
        out_shape=[
            jax.ShapeDtypeStruct((depth, bsz, n), F32),
            jax.ShapeDtypeStruct((depth, 3, hk), F32),
        ],
        compiler_params=pltpu.CompilerParams(
            dimension_semantics=("arbitrary", "arbitrary"), vmem_limit_bytes=VMEM_LIMIT),
        name="adaln_params",
    )(c, w_ada, b_ada.reshape(depth, 1, n), hg_lb_logits)


def _kv_variants(t, low_half):
    sw = pltpu.roll(t, ATT_HDIM, 1)
    return (jnp.where(low_half, t, 0.0).astype(BF16), jnp.where(low_half, 0.0, sw).astype(BF16),
            jnp.where(low_half, sw, 0.0).astype(BF16), jnp.where(low_half, 0.0, t).astype(BF16))


class _Columns:
    def __init__(self, group_fn, width):
        self.group_fn, self.width, self.pieces = group_fn, width, {}

    def __getitem__(self, idx):
        rows, cols = idx
        piece, lo = divmod(cols.start, self.width)
        assert cols.stop - cols.start <= self.width - lo, "column slice straddles two groups"
        if piece not in self.pieces:
            self.pieces[piece] = self.group_fn(piece * self.width)
        return self.pieces[piece][rows, lo:lo + cols.stop - cols.start]


def _mix_kernel(x_ref, mod_ref, gpre_ref, gpost_ref, win32_ref, lbp_ref, hgn_ref, poolw_ref,
                pools_ref, sink_ref, wout32_ref, eblk_ref, cum_ref,
                o_ref,
                win_ref, wout_ref, st_ref, kv_ref, ptail_ref, qf_s, b_s, kk_s, vi_s, oa_s):
    s_idx = pl.program_id(1)
    tile = x_ref.shape[1]

    @pl.when((pl.program_id(0) == 0) & (s_idx == 0))
    def _():
        win_ref[...] = win32_ref[...].astype(BF16)
        wout_ref[...] = wout32_ref[...].astype(BF16)

    @pl.when(s_idx == 0)
    def _():
        st_ref[...] = jnp.zeros_like(st_ref)
        kv_ref[...] = jnp.zeros_like(kv_ref)
        ptail_ref[0:POOL_TAIL, :] = jnp.zeros((POOL_TAIL, POOL_WIDTH), F32)

    x = x_ref[0]
    gt1 = mod_ref[0, 2:3, :]
    ms = jnp.mean(x * x, axis=-1, keepdims=True)
    hb = ((x * lax.rsqrt(ms + EPS)) * (gpre_ref[...] * (1.0 + mod_ref[0, 1:2, :])) + mod_ref[0, 0:1, :]).astype(BF16)
    proj = _Columns(lambda c: _dot(hb, win_ref[0, :, c:c + PROJ_GROUP]), PROJ_GROUP)

    eblk = eblk_ref[...]
    emask = eblk > 0

    q = proj[:, OFF_HQ:OFF_HQ + HG_WIDTH]
    z = proj[:, OFF_HF:OFF_HF + HG_WIDTH]
    qf = q * _sigmoid(q) * (HG_DK ** -0.5)
    log_sig = jnp.minimum(z, 0.0) - jnp.log(1.0 + jnp.exp(-jnp.abs(z)))
    la = lbp_ref[0, 0:1, :]
    lc = lbp_ref[0, 1:2, :] + log_sig
    log_f = jnp.maximum(la, lc) + jnp.log(1.0 + jnp.exp(-jnp.abs(la - lc)))
    kk = lbp_ref[0, 2:3, :] * _sigmoid(-z)
    vi = proj[:, OFF_HI:OFF_HI + HG_WIDTH]
    gate = proj[:, OFF_HG:OFF_HG + HG_WIDTH]
    out_gain = hgn_ref[...] * (gate * _sigmoid(gate))
    parts = jnp.concatenate(_split3(log_f), axis=1)
    cums = jnp.concatenate(
        [_dot(cum_ref[...], parts[c:c + HG_CHUNK]) for c in range(0, tile, HG_CHUNK)], axis=0)
    w = HG_WIDTH
    b_loc = cums[:, 0:w] + cums[:, w:2 * w] + cums[:, 2 * w:3 * w]

    sub = HG_SUB
    spread = None
    for i in range(tile // sub):
        bl = b_loc[i * sub:(i + 1) * sub]
        sp = jnp.max(jnp.abs(bl - bl[sub // 2 - 1:sub // 2]))
        spread = sp if spread is None else jnp.maximum(spread, sp)
    factorisable = spread <= HG_SAFE_EXP

    def pairwise_path():
        qf_s[...] = qf
        b_s[...] = b_loc
        kk_s[...] = kk
        vi_s[...] = vi
        rows16 = lax.broadcasted_iota(jnp.int32, (HG_BLOCK, HG_WIDTH), 0)
        per_chunk = HG_CHUNK // HG_BLOCK

        def hg_step(i, carry):
            r0 = pl.multiple_of(i * HG_BLOCK, HG_BLOCK)
            qb = qf_s[pl.ds(r0, HG_BLOCK), :]
            bb = b_s[pl.ds(r0, HG_BLOCK), :]
            kb = kk_s[pl.ds(r0, HG_BLOCK), :]
            b_prev = jnp.where(i % per_chunk == 0, 0.0, b_s[pl.ds(jnp.maximum(r0 - 1, 0), 1), :])
            b_end = b_s[pl.ds(r0 + HG_BLOCK - 1, 1), :]
            pieces = []
            for j in range(HG_BLOCK):
                bj = b_s[pl.ds(r0 + j, 1), :]
                kj_row = kk_s[pl.ds(r0 + j, 1), :]
                d = jnp.where(rows16 >= j, (qb * kj_row) * jnp.exp(bb - bj), 0.0)
                pieces.append(d.astype(BF16))
            dall = jnp.concatenate(pieces, axis=0)
            sall = _dot(dall, eblk)
            acc = jnp.zeros((HG_BLOCK, HG_WIDTH), F32)
            for j in range(HG_BLOCK):
                vj = vi_s[pl.ds(r0 + j, 1), :]
                acc = acc + sall[j * HG_BLOCK:(j + 1) * HG_BLOCK, :] * vj
            st_old = st_ref[...]
            qs_b = (qb * jnp.exp(bb - b_prev)).astype(BF16)
            ke_b = (kb * jnp.exp(b_end - bb)).astype(BF16)
            oa_s[pl.ds(r0, HG_BLOCK), :] = acc + _dot_nt(qs_b, st_old.astype(BF16))
            vb_b = vi_s[pl.ds(r0, HG_BLOCK), :].astype(BF16)
            st_ref[...] = st_old * jnp.exp(b_end - b_prev) + jnp.where(emask, _dot_tn(vb_b, ke_b), 0.0)
            return carry

        lax.fori_loop(0, tile // HG_BLOCK, hg_step, 0)

    b2 = b_loc * LOG2E
    n_sub = HG_CHUNK // sub
    half = HG_CHUNK // 2
    lane_head = lax.broadcasted_iota(jnp.int32, (HG_CHUNK, HG_WIDTH), 1) // HG_DK
    head_keep = [jnp.where(lane_head == hd, 1.0, 0.0).astype(BF16) for hd in range(HG_HEADS)]

    def stack_heads(a):
        ab = a.astype(BF16)
        return jnp.concatenate([ab * keep for keep in head_keep], axis=0)

    ti = lax.broadcasted_iota(jnp.int32, (HG_HEADS * HG_CHUNK, HG_CHUNK), 0) % HG_CHUNK
    tj = lax.broadcasted_iota(jnp.int32, (HG_HEADS * HG_CHUNK, HG_CHUNK), 1)
    m_same32 = (ti // sub == tj // sub) & (ti >= tj)
    m_same64 = (ti // (2 * sub) == tj // (2 * sub)) & (ti // sub > tj // sub)
    zsub = jnp.zeros((sub, HG_WIDTH), F32)
    zhalf = jnp.zeros((half, HG_WIDTH), F32)
    def hg_chunk(c, st):
        r = slice(c * HG_CHUNK, (c + 1) * HG_CHUNK)
        bl = b2[r]
        qc = qf[r]
        kc = kk[r]
        vc = vi[r]
        blk = lambda a, i: a[i * sub:(i + 1) * sub]
        row = lambda i: bl[i:i + 1]
        mids = [row(i * sub + sub // 2 - 1) for i in range(n_sub)]
        q_a = jnp.concatenate([blk(qc, i) * jnp.exp2(blk(bl, i) - mids[i]) for i in range(n_sub)], axis=0)
        k_a = jnp.concatenate([blk(kc, i) * jnp.exp2(mids[i] - blk(bl, i)) for i in range(n_sub)], axis=0)
        q_b = jnp.concatenate(
            [zsub if i % 2 == 0 else blk(qc, i) * jnp.exp2(blk(bl, i) - row(i * sub - 1)) for i in range(n_sub)],
            axis=0)
        k_b = jnp.concatenate(
            [blk(kc, i) * jnp.exp2(row((i + 1) * sub - 1) - blk(bl, i)) if i % 2 == 0 else zsub
             for i in range(n_sub)], axis=0)
        q_c = jnp.concatenate([zhalf, qc[half:] * jnp.exp2(bl[half:] - row(half - 1))], axis=0)
        k_c = jnp.concatenate([kc[:half] * jnp.exp2(row(half - 1) - bl[:half]), zhalf], axis=0)
        s_a = _dot_nt(stack_heads(q_a), k_a.astype(BF16))
        s_b = _dot_nt(stack_heads(q_b), k_b.astype(BF16))
        s_c = _dot_nt(stack_heads(q_c), k_c.astype(BF16))
        p = jnp.where(m_same32, s_a, jnp.where(m_same64, s_b, s_c)).astype(BF16)
        pcat = jnp.concatenate([p[hd * HG_CHUNK:(hd + 1) * HG_CHUNK] for hd in range(HG_HEADS)], axis=1)
        o_intra = _dot(pcat, stack_heads(vc))
        b_end = row(HG_CHUNK - 1)
        qs = (qc * jnp.exp2(bl)).astype(BF16)
        ke = (kc * jnp.exp2(b_end - bl)).astype(BF16)
        o_chunk = o_intra + _dot_nt(qs, st.astype(BF16))
        return o_chunk, st * jnp.exp2(b_end) + jnp.where(emask, _dot_tn(vc.astype(BF16), ke), 0.0)

    def pool_group():
        pv = proj[:, OFF_PV:OFF_PV + POOL_WIDTH]
        ptail_ref[POOL_TAIL:POOL_TAIL + tile, :] = pv
        lane_p = lax.broadcasted_iota(jnp.int32, (1, POOL_WIDTH), 1)
        win_l = jnp.left_shift(2, lane_p // POOL_GDIM)
        groups_per_col = 128 // POOL_GDIM
        first_group = lax.broadcasted_iota(jnp.int32, (tile, 128), 1) < POOL_GDIM
        cols = []
        for col in range(POOL_WIDTH // 128):
            cur = ptail_ref[:, col * 128:(col + 1) * 128]
            start = 0
            sums = []
            for level in range((col + 1) * groups_per_col):
                shift = 1 << level
                rows = cur.shape[0]
                cur = cur[SUBLANES:] + cur[SUBLANES - shift:rows - shift]
                start += SUBLANES
                sums.append(cur[POOL_TAIL - start:])
            cols.append(jnp.where(first_group, sums[-2], sums[-1]))
        acc_p = jnp.concatenate(cols, axis=1)
        t_glob = s_idx * tile + lax.broadcasted_iota(jnp.int32, (tile, POOL_WIDTH), 0)
        cnt = jnp.minimum(t_glob + 1, win_l).astype(F32)
        pooled = acc_p / cnt - pv
        ptail_ref[0:POOL_TAIL, :] = ptail_ref[pl.ds(tile, POOL_TAIL), :]
        return _dot(pooled.astype(BF16), poolw_ref[...]) * pools_ref[...]

    o_b = pool_group()

    group = ATT_HEADS // ATT_KV_HEADS
    pair_rows = (group // 2) * ATT_BLOCK
    low_half = lax.broadcasted_iota(jnp.int32, (ATT_BLOCK, ATT_KV_WIDTH), 1) < ATT_HDIM
    qi = lax.broadcasted_iota(jnp.int32, (pair_rows, ATT_BLOCK), 0) % ATT_BLOCK
    kj = lax.broadcasted_iota(jnp.int32, (pair_rows, ATT_BLOCK), 1)
    upper = kj > qi
    first_pair = lax.broadcasted_iota(jnp.int32, (pair_rows, 1), 0) < ATT_BLOCK
    q_scale = (ATT_HDIM ** -0.5) * LOG2E
    def att_block(a, prev):
        r = slice(a * ATT_BLOCK, (a + 1) * ATT_BLOCK)
        cur_k = _kv_variants(proj[r, OFF_AK:OFF_AK + ATT_KV_WIDTH], low_half)
        cur_v = _kv_variants(proj[r, OFF_AV:OFF_AV + ATT_KV_WIDTH], low_half)
        if prev is None:
            prev_k = tuple(kv_ref[n] for n in range(4))
            prev_v = tuple(kv_ref[4 + n] for n in range(4))
            no_prev = jnp.where(upper & (s_idx == 0), -jnp.inf, 0.0)
        else:
            prev_k, prev_v = prev
            no_prev = None
        pair_out = []
        for g in range(ATT_KV_HEADS):
            c0 = OFF_AQ + g * group * ATT_HDIM
            q2 = jnp.concatenate(
                [(proj[r, c0 + p * 2 * ATT_HDIM:c0 + (p + 1) * 2 * ATT_HDIM] * q_scale).astype(BF16)
                 for p in range(group // 2)], axis=0)
            kcat = jnp.concatenate([prev_k[2 * g], cur_k[2 * g], prev_k[2 * g + 1], cur_k[2 * g + 1]], axis=0)
            s_all = _dot_nt(q2, kcat)
            o_g = None
            for half in range(2):
                s_prev = s_all[:, (2 * half) * ATT_BLOCK:(2 * half + 1) * ATT_BLOCK]
                s_cur = s_all[:, (2 * half + 1) * ATT_BLOCK:(2 * half + 2) * ATT_BLOCK]
                if no_prev is not None:
                    s_prev = s_prev + no_prev
                s = jnp.where(upper, s_prev, s_cur)
                head0 = g * group + half
                sink = jnp.where(first_pair, sink_ref[head0], sink_ref[head0 + 2]) * LOG2E
                m = jnp.maximum(jnp.max(s, axis=-1, keepdims=True), sink)
                pexp = jnp.exp2(s - m)
                denom = jnp.sum(pexp, axis=-1, keepdims=True) + jnp.exp2(sink - m)
                pcat = jnp.concatenate([jnp.where(upper, pexp, 0.0).astype(BF16),
                                        jnp.where(upper, 0.0, pexp).astype(BF16)], axis=1)
                vcat = jnp.concatenate([prev_v[2 * g + half], cur_v[2 * g + half]], axis=0)
                o_h = _dot(pcat, vcat) * (1.0 / denom)
                o_g = o_h if o_g is None else o_g + o_h
            for p in range(group // 2):
                pair_out.append(o_g[p * ATT_BLOCK:(p + 1) * ATT_BLOCK])
        return jnp.concatenate(pair_out, axis=1), (cur_k, cur_v)

    assert HG_CHUNK == ATT_BLOCK
    st = st_ref[...]
    prev = None
    oa_chunks, o_c_blocks = [], []
    for j in range(tile // ATT_BLOCK):
        o_chunk, st = hg_chunk(j, st)
        oa_chunks.append(o_chunk)
        o_block, prev = att_block(j, prev)
        o_c_blocks.append(o_block)
    oa_fast = jnp.concatenate(oa_chunks, axis=0)
    st_fast = st
    for n in range(4):
        kv_ref[n] = prev[0][n]
        kv_ref[4 + n] = prev[1][n]
    o_c = jnp.concatenate(o_c_blocks, axis=0)

    mix_bc = _dot(jnp.concatenate([o_b.astype(BF16), o_c.astype(BF16)], axis=1), wout_ref[0, HG_WIDTH:, :])

    oa_s[...] = oa_fast

    @pl.when(factorisable)
    def _():
        st_ref[...] = st_fast

    pl.when(jnp.logical_not(factorisable))(pairwise_path)

    oa = oa_s[...]
    ms_h = _dot((oa * oa).astype(BF16), eblk) * (1.0 / HG_DV)
    o_a = oa * lax.rsqrt(ms_h + EPS) * out_gain

    mix = mix_bc + _dot(o_a.astype(BF16), wout_ref[0, 0:HG_WIDTH, :])
    msm = jnp.mean(mix * mix, axis=-1, keepdims=True)
    o_ref[0] = x_ref[0] + (mix * lax.rsqrt(msm + EPS)) * (gt1 * gpost_ref[...])


def _const_spec(shape):
    nd = len(shape)
    return pl.BlockSpec(shape, lambda *_, _nd=nd: (0,) * _nd, pipeline_mode=pl.Buffered(1))


def _layer_spec(shape, layer):
    nd = len(shape)
    return pl.BlockSpec((1,) + tuple(shape[1:]), lambda *_, _nd=nd, _l=layer: (_l,) + (0,) * (_nd - 1),
                        pipeline_mode=pl.Buffered(1))


def _mix_call(layer, x, mod, g_pre, g_post, w_in, lbp, hg_norm_t, pool_bd, pool_scale, sinks, w_out, eblk, cum):
    bsz, seq, d = x.shape
    tile = min(MIX_TILE, seq)
    grid = (bsz, seq // tile)
    row = lambda v: v.reshape(1, -1)
    in_specs = [
        pl.BlockSpec((1, tile, d), lambda b, s: (b, s, 0)),
        pl.BlockSpec((1, 6, d), lambda b, s: (b, 0, 0)),
        _const_spec((1, d)), _const_spec((1, d)),
        _layer_spec(w_in.shape, layer),
        _layer_spec(lbp.shape, layer),
        _const_spec((1, HG_WIDTH)),
        _const_spec(pool_bd.shape),
        _const_spec((1, POOL_WIDTH)),
        pl.BlockSpec(memory_space=pltpu.SMEM),
        _layer_spec(w_out.shape, layer),
        _const_spec(eblk.shape),
        _const_spec(cum.shape),
    ]
    scratch = [
        pltpu.VMEM((1,) + tuple(w_in.shape[1:]), BF16),
        pltpu.VMEM((1,) + tuple(w_out.shape[1:]), BF16),
        pltpu.VMEM((HG_WIDTH, HG_WIDTH), F32),
        pltpu.VMEM((8, ATT_BLOCK, ATT_KV_WIDTH), BF16),
        pltpu.VMEM((POOL_TAIL + tile, POOL_WIDTH), F32),
        pltpu.VMEM((tile, HG_WIDTH), F32),
        pltpu.VMEM((tile, HG_WIDTH), F32),
        pltpu.VMEM((tile, HG_WIDTH), F32),
        pltpu.VMEM((tile, HG_WIDTH), F32),
        pltpu.VMEM((tile, HG_WIDTH), F32),
    ]
    return pl.pallas_call(
        _mix_kernel,
        grid=grid,
        in_specs=in_specs,
        out_specs=pl.BlockSpec((1, tile, d), lambda b, s: (b, s, 0)),
        out_shape=jax.ShapeDtypeStruct(x.shape, F32),
        scratch_shapes=scratch,
        compiler_params=pltpu.CompilerParams(
            dimension_semantics=("arbitrary", "arbitrary"), vmem_limit_bytes=VMEM_LIMIT),
        name="token_mixer",
    )(x, mod, row(g_pre), row(g_post), w_in, lbp, row(hg_norm_t), pool_bd, row(pool_scale), sinks,
      w_out, eblk, cum)


def _gelu_tanh(x):
    c = 0.7978845608028654
    return (0.5 * x) * (1.0 + jnp.tanh(x * (c + (c * 0.044715) * (x * x))))


def _ffn_kernel(x_ref, mod_ref, gpre_ref, gpost_ref, wup_ref, cw_ref, cb_ref, wdown_ref,
                o_ref,
                tail_ref, ubuf_ref, act_ref):
    s_idx = pl.program_id(1)
    tile = x_ref.shape[1]
    d_model = x_ref.shape[2]

    @pl.when(s_idx == 0)
    def _():
        tail_ref[...] = jnp.zeros_like(tail_ref)

    x = x_ref[0]
    ms = jnp.mean(x * x, axis=-1, keepdims=True)
    gain = gpre_ref[...] * (1.0 + mod_ref[0, 4:5, :])
    hb = ((x * lax.rsqrt(ms + EPS)) * gain + mod_ref[0, 3:4, :]).astype(BF16)

    def up_cols(off, slot):
        ubuf = ubuf_ref.at[slot]
        u = _dot(hb, wup_ref[0, :, off:off + FFN_CHUNK])
        ubuf[0:SUBLANES, :] = tail_ref[:, off:off + FFN_CHUNK]
        ubuf[SUBLANES:SUBLANES + tile, :] = u
        tail_ref[:, off:off + FFN_CHUNK] = ubuf[pl.ds(tile, SUBLANES), :]
        return u

    def conv_cols(u, off, slot):
        ubuf = ubuf_ref.at[slot]
        w0 = cw_ref[0:1, off:off + FFN_CHUNK]
        w1 = cw_ref[1:2, off:off + FFN_CHUNK]
        w2 = cw_ref[2:3, off:off + FFN_CHUNK]
        return (w0 * ubuf[pl.ds(SUBLANES - 2, tile), :] + w1 * ubuf[pl.ds(SUBLANES - 1, tile), :]
                + w2 * u + cb_ref[0:1, off:off + FFN_CHUNK])

    n_chunks = D_FF // FFN_CHUNK
    pending = (up_cols(0, 0), up_cols(D_FF, 1))
    for c in range(n_chunks):
        u_gate, u_val = pending
        if c + 1 < n_chunks:
            nxt = 2 * ((c + 1) % 2)
            pending = (up_cols((c + 1) * FFN_CHUNK, nxt), up_cols(D_FF + (c + 1) * FFN_CHUNK, nxt + 1))
        if c == n_chunks - FFN_EARLY_DOWN:
            k_early = c * FFN_CHUNK
            ys_early = [_dot(act_ref[:, 0:k_early], wdown_ref[0, 0:k_early, n:n + FFN_CHUNK])
                        for n in range(0, d_model, FFN_CHUNK)]
        slot = 2 * (c % 2)
        g = conv_cols(u_gate, c * FFN_CHUNK, slot)
        v = conv_cols(u_val, D_FF + c * FFN_CHUNK, slot + 1)
        act_ref[:, c * FFN_CHUNK:(c + 1) * FFN_CHUNK] = (_gelu_tanh(g) * v).astype(BF16)

    ys = [y_early + _dot(act_ref[:, k_early:], wdown_ref[0, k_early:, n:n + FFN_CHUNK])
          for y_early, n in zip(ys_early, range(0, d_model, FFN_CHUNK))]
    ssq = None
    for y in ys:
        part = jnp.sum(y * y, axis=-1, keepdims=True)
        ssq = part if ssq is None else ssq + part
    rs = lax.rsqrt(ssq * (1.0 / d_model) + EPS)
    out_gain = mod_ref[0, 5:6, :] * gpost_ref[...]
    for g, y in enumerate(ys):
        cols = slice(g * FFN_CHUNK, (g + 1) * FFN_CHUNK)
        o_ref[0, :, cols] = x_ref[0, :, cols] + (y * rs) * out_gain[:, cols]


def _ffn_call(layer, x, mod, g_pre, g_post, w_up, conv_w, conv_b, w_down):
    bsz, seq, d = x.shape
    tile = min(FFN_TILE, seq)
    grid = (bsz, seq // tile)
    row = lambda v: v.reshape(1, -1)
    in_specs = [
        pl.BlockSpec((1, tile, d), lambda b, s: (b, s, 0)),
        pl.BlockSpec((1, 6, d), lambda b, s: (b, 0, 0)),
        _const_spec((1, d)), _const_spec((1, d)),
        _layer_spec(w_up.shape, layer),
        _const_spec(conv_w.shape),
        _const_spec((1, 2 * D_FF)),
        _layer_spec(w_down.shape, layer),
    ]
    scratch = [
        pltpu.VMEM((SUBLANES, 2 * D_FF), F32),
        pltpu.VMEM((4, SUBLANES + tile, FFN_CHUNK), F32),
        pltpu.VMEM((tile, D_FF), BF16),
    ]
    return pl.pallas_call(
        _ffn_kernel,
        grid=grid,
        in_specs=in_specs,
        out_specs=pl.BlockSpec((1, tile, d), lambda b, s: (b, s, 0)),
        out_shape=jax.ShapeDtypeStruct(x.shape, F32),
        scratch_shapes=scratch,
        compiler_params=pltpu.CompilerParams(
            dimension_semantics=("arbitrary", "arbitrary"), vmem_limit_bytes=VMEM_LIMIT),
        name="conv_ffn",
    )(x, mod, row(g_pre), row(g_post), w_up, conv_w, row(conv_b), w_down)


def _block_diag(blocks):
    g, c, _ = blocks.shape
    eye = jnp.eye(g, dtype=blocks.dtype)
    return (eye[:, None, :, None] * blocks[:, :, None, :]).reshape(g * c, g * c)


def _cum_matrix(rows):
    r = jnp.arange(rows)
    return (r[None, :] <= r[:, None]).astype(BF16)


def kernel(x, c, w_ada, b_ada, g_pre_mix, g_post_mix, w_in, hg_lb_logits, hg_norm, pool_w, pool_scale,
           attn_sinks, w_out, g_pre_ffn, g_post_ffn, w_up, conv_w, conv_b, w_down):
    depth = w_ada.shape[0]
    bsz, seq, d = x.shape
    mod_all, lbp_all = _ada_call(c, w_ada, b_ada, hg_lb_logits)
    mod_all = mod_all.reshape(depth, bsz, 6, d)
    eblk = _block_diag(jnp.ones((HG_HEADS, HG_DK, HG_DV), BF16))
    cum = _cum_matrix(HG_CHUNK)
    w_up_b, w_down_b = w_up.astype(BF16), w_down.astype(BF16)
    for l in range(depth):
        x = _mix_call(
            l, x, mod_all[l], g_pre_mix[l], g_post_mix[l], w_in, lbp_all,
            jnp.tile(hg_norm[l], HG_HEADS), _block_diag(pool_w[l]).astype(BF16), pool_scale[l],
            attn_sinks[l], w_out, eblk, cum)
        x = _ffn_call(
            l, x, mod_all[l], g_pre_ffn[l], g_post_ffn[l], w_up_b, conv_w[l], conv_b[l], w_down_b)
    return x
```

```python
import functools

import jax
import jax.numpy as jnp
from jax import lax
from jax.experimental import pallas as pl
from jax.experimental.pallas import tpu as pltpu

F32 = jnp.float32
BF16 = jnp.bfloat16

D_MODEL = 1024
HG_HEADS = 4
HG_DK = 64
HG_DV = 64
HG_WIDTH = HG_HEADS * HG_DV
POOL_WINDOWS = (2, 4, 8, 16)
POOL_GDIM = 64
POOL_WIDTH = len(POOL_WINDOWS) * POOL_GDIM
POOL_TAIL = 32
ATT_HEADS = 8
ATT_KV_HEADS = 2
ATT_HDIM = 64
ATT_WIDTH = ATT_HEADS * ATT_HDIM
ATT_KV_WIDTH = ATT_KV_HEADS * ATT_HDIM
WINDOW = 128
ATT_BLOCK = WINDOW
D_FF = 2816
CONV_WIDTH = 3
EPS = 1e-6
LOG2E = 1.4426950408889634
D_IN = 4 * HG_WIDTH + POOL_WIDTH + ATT_WIDTH + 2 * ATT_KV_WIDTH
OFF_HQ = 0
OFF_HF = OFF_HQ + HG_WIDTH
OFF_HI = OFF_HF + HG_WIDTH
OFF_HG = OFF_HI + HG_WIDTH
OFF_PV = OFF_HG + HG_WIDTH
OFF_AQ = OFF_PV + POOL_WIDTH
OFF_AK = OFF_AQ + ATT_WIDTH
OFF_AV = OFF_AK + ATT_KV_WIDTH

HG_CHUNK = 128
HG_SUB = 32
HG_BLOCK = 16
HG_SAFE_EXP = 60.0
MIX_TILE = 512
PROJ_GROUP = 1024
FFN_TILE = 512
FFN_CHUNK = 256
FFN_STAGE = 512
FFN_EARLY_DOWN = 2
ADA_TILE = 3072
SUBLANES = 8
VMEM_LIMIT = 56 * 1024 * 1024


def _sigmoid(x):
    return 1.0 / (1.0 + jnp.exp(-x))


def _split3(x):
    hi = x.astype(BF16)
    r1 = x - hi.astype(F32)
    mid = r1.astype(BF16)
    lo = (r1 - mid.astype(F32)).astype(BF16)
    return hi, mid, lo


def _dot(a, b):
    return jnp.dot(a, b, preferred_element_type=F32)


def _dot_nt(a, b):
    return lax.dot_general(a, b, (((1,), (1,)), ((), ())), preferred_element_type=F32)


def _dot_tn(a, b):
    return lax.dot_general(a, b, (((0,), (0,)), ((), ())), preferred_element_type=F32)


def _ada_kernel(c_ref, w_ref, b_ref, lbl_ref, mod_ref, lbp_ref):
    c = c_ref[...]
    ca = c * _sigmoid(c)
    acc = None
    wb = w_ref[0].astype(BF16)
    for piece in _split3(ca):
        t = _dot(piece, wb)
        acc = t if acc is None else acc + t
    mod_ref[0] = acc + b_ref[0]

    logits = lbl_ref[...]
    depth = logits.shape[0]
    mx = jnp.max(logits, axis=0, keepdims=True)
    ex = jnp.exp(logits - mx)
    sm = ex / jnp.sum(ex, axis=0, keepdims=True)
    run = sm[0:1]
    first = run
    for l in range(depth):
        if l > 0:
            run = run + sm[l:l + 1]
        lb = run - first
        lbp_ref[l, 0:1, :] = jnp.log(lb)
        lbp_ref[l, 1:2, :] = jnp.log1p(-lb)
        lbp_ref[l, 2:3, :] = 1.0 - lb


def _ada_call(c, w_ada, b_ada, hg_lb_logits):
    depth, d, n = w_ada.shape
    bsz = c.shape[0]
    hk = hg_lb_logits.shape[1]
    grid = (depth, n // ADA_TILE)
    return pl.pallas_call(
        _ada_kernel,
        grid=grid,
        in_specs=[
            pl.BlockSpec((bsz, d), lambda l, j: (0, 0)),
            pl.BlockSpec((1, d, ADA_TILE), lambda l, j: (l, 0, j)),
            pl.BlockSpec((1, 1, ADA_TILE), lambda l, j: (l, 0, j)),
            pl.BlockSpec((depth, hk), lambda l, j: (0, 0)),
        ],
        out_specs=[
            pl.BlockSpec((1, bsz, ADA_TILE), lambda l, j: (l, 0, j)),
            pl.BlockSpec((depth, 3, hk), lambda l, j: (0, 0, 0)),
        ],
        out_shape=[
            jax.ShapeDtypeStruct((depth, bsz, n), F32),
            jax.ShapeDtypeStruct((depth, 3, hk), F32),
        ],
        compiler_params=pltpu.CompilerParams(
            dimension_semantics=("arbitrary", "arbitrary"), vmem_limit_bytes=VMEM_LIMIT),
        name="adaln_params",
    )(c, w_ada, b_ada.reshape(depth, 1, n), hg_lb_logits)


def _kv_variants(t, low_half):
    sw = pltpu.roll(t, ATT_HDIM, 1)
    return (jnp.where(low_half, t, 0.0).astype(BF16), jnp.where(low_half, 0.0, sw).astype(BF16),
            jnp.where(low_half, sw, 0.0).astype(BF16), jnp.where(low_half, 0.0, t).astype(BF16))


class _Columns:
    def __init__(self, group_fn, width):
        self.group_fn, self.width, self.pieces = group_fn, width, {}

    def __getitem__(self, idx):
        rows, cols = idx
        piece, lo = divmod(cols.start, self.width)
        assert cols.stop - cols.start <= self.width - lo, "column slice straddles two groups"
        if piece not in self.pieces:
            self.pieces[piece] = self.group_fn(piece * self.width)
        return self.pieces[piece][rows, lo:lo + cols.stop - cols.start]


def _mix_kernel(x_ref, mod_ref, gpre_ref, gpost_ref, win32_ref, lbp_ref, hgn_ref, poolw_ref,
                pools_ref, sink_ref, wout32_ref, eblk_ref, cum_ref,
                o_ref,
                win_ref, wout_ref, st_ref, kv_ref, ptail_ref, qf_s, b_s, kk_s, vi_s, oa_s):
    s_idx = pl.program_id(1)
    tile = x_ref.shape[1]

    @pl.when((pl.program_id(0) == 0) & (s_idx == 0))
    def _():
        win_ref[...] = win32_ref[...].astype(BF16)
        wout_ref[...] = wout32_ref[...].astype(BF16)

    @pl.when(s_idx == 0)
    def _():
        st_ref[...] = jnp.zeros_like(st_ref)
        kv_ref[...] = jnp.zeros_like(kv_ref)
        ptail_ref[0:POOL_TAIL, :] = jnp.zeros((POOL_TAIL, POOL_WIDTH), F32)

    x = x_ref[0]
    gt1 = mod_ref[0, 2:3, :]
    ms = jnp.mean(x * x, axis=-1, keepdims=True)
    hb = ((x * lax.rsqrt(ms + EPS)) * (gpre_ref[...] * (1.0 + mod_ref[0, 1:2, :])) + mod_ref[0, 0:1, :]).astype(BF16)
    proj = _Columns(lambda c: _dot(hb, win_ref[0, :, c:c + PROJ_GROUP]), PROJ_GROUP)

    eblk = eblk_ref[...]
    emask = eblk > 0

    q = proj[:, OFF_HQ:OFF_HQ + HG_WIDTH]
    z = proj[:, OFF_HF:OFF_HF + HG_WIDTH]
    qf = q * _sigmoid(q) * (HG_DK ** -0.5)
    log_sig = jnp.minimum(z, 0.0) - jnp.log(1.0 + jnp.exp(-jnp.abs(z)))
    la = lbp_ref[0, 0:1, :]
    lc = lbp_ref[0, 1:2, :] + log_sig
    log_f = jnp.maximum(la, lc) + jnp.log(1.0 + jnp.exp(-jnp.abs(la - lc)))
    kk = lbp_ref[0, 2:3, :] * _sigmoid(-z)
    vi = proj[:, OFF_HI:OFF_HI + HG_WIDTH]
    gate = proj[:, OFF_HG:OFF_HG + HG_WIDTH]
    out_gain = hgn_ref[...] * (gate * _sigmoid(gate))
    parts = jnp.concatenate(_split3(log_f), axis=1)
    cums = jnp.concatenate(
        [_dot(cum_ref[...], parts[c:c + HG_CHUNK]) for c in range(0, tile, HG_CHUNK)], axis=0)
    w = HG_WIDTH
    b_loc = cums[:, 0:w] + cums[:, w:2 * w] + cums[:, 2 * w:3 * w]

    sub = HG_SUB
    spread = None
    for i in range(tile // sub):
        bl = b_loc[i * sub:(i + 1) * sub]
        sp = jnp.max(jnp.abs(bl - bl[sub // 2 - 1:sub // 2]))
        spread = sp if spread is None else jnp.maximum(spread, sp)
    factorisable = spread <= HG_SAFE_EXP

    def pairwise_path():
        qf_s[...] = qf
        b_s[...] = b_loc
        kk_s[...] = kk
        vi_s[...] = vi
        rows16 = lax.broadcasted_iota(jnp.int32, (HG_BLOCK, HG_WIDTH), 0)
        per_chunk = HG_CHUNK // HG_BLOCK

        def hg_step(i, carry):
            r0 = pl.multiple_of(i * HG_BLOCK, HG_BLOCK)
            qb = qf_s[pl.ds(r0, HG_BLOCK), :]
            bb = b_s[pl.ds(r0, HG_BLOCK), :]
            kb = kk_s[pl.ds(r0, HG_BLOCK), :]
            b_prev = jnp.where(i % per_chunk == 0, 0.0, b_s[pl.ds(jnp.maximum(r0 - 1, 0), 1), :])
            b_end = b_s[pl.ds(r0 + HG_BLOCK - 1, 1), :]
            pieces = []
            for j in range(HG_BLOCK):
                bj = b_s[pl.ds(r0 + j, 1), :]
                kj_row = kk_s[pl.ds(r0 + j, 1), :]
                d = jnp.where(rows16 >= j, (qb * kj_row) * jnp.exp(bb - bj), 0.0)
                pieces.append(d.astype(BF16))
            dall = jnp.concatenate(pieces, axis=0)
            sall = _dot(dall, eblk)
            acc = jnp.zeros((HG_BLOCK, HG_WIDTH), F32)
            for j in range(HG_BLOCK):
                vj = vi_s[pl.ds(r0 + j, 1), :]
                acc = acc + sall[j * HG_BLOCK:(j + 1) * HG_BLOCK, :] * vj
            st_old = st_ref[...]
            qs_b = (qb * jnp.exp(bb - b_prev)).astype(BF16)
            ke_b = (kb * jnp.exp(b_end - bb)).astype(BF16)
            oa_s[pl.ds(r0, HG_BLOCK), :] = acc + _dot_nt(qs_b, st_old.astype(BF16))
            vb_b = vi_s[pl.ds(r0, HG_BLOCK), :].astype(BF16)
            st_ref[...] = st_old * jnp.exp(b_end - b_prev) + jnp.where(emask, _dot_tn(vb_b, ke_b), 0.0)
            return carry

        lax.fori_loop(0, tile // HG_BLOCK, hg_step, 0)

    b2 = b_loc * LOG2E
    n_sub = HG_CHUNK // sub
    half = HG_CHUNK // 2
    lane_head = lax.broadcasted_iota(jnp.int32, (HG_CHUNK, HG_WIDTH), 1) // HG_DK
    head_keep = [jnp.where(lane_head == hd, 1.0, 0.0).astype(BF16) for hd in range(HG_HEADS)]

    def stack_heads(a):
        ab = a.astype(BF16)
        return jnp.concatenate([ab * keep for keep in head_keep], axis=0)

    ti = lax.broadcasted_iota(jnp.int32, (HG_HEADS * HG_CHUNK, HG_CHUNK), 0) % HG_CHUNK
    tj = lax.broadcasted_iota(jnp.int32, (HG_HEADS * HG_CHUNK, HG_CHUNK), 1)
    m_same32 = (ti // sub == tj // sub) & (ti >= tj)
    m_same64 = (ti // (2 * sub) == tj // (2 * sub)) & (ti // sub > tj // sub)
    zsub = jnp.zeros((sub, HG_WIDTH), F32)
    zhalf = jnp.zeros((half, HG_WIDTH), F32)
    def hg_chunk(c, st):
        r = slice(c * HG_CHUNK, (c + 1) * HG_CHUNK)
        bl = b2[r]
        qc = qf[r]
        kc = kk[r]
        vc = vi[r]
        blk = lambda a, i: a[i * sub:(i + 1) * sub]
        row = lambda i: bl[i:i + 1]
        mids = [row(i * sub + sub // 2 - 1) for i in range(n_sub)]
        q_a = jnp.concatenate([blk(qc, i) * jnp.exp2(blk(bl, i) - mids[i]) for i in range(n_sub)], axis=0)
        k_a = jnp.concatenate([blk(kc, i) * jnp.exp2(mids[i] - blk(bl, i)) for i in range(n_sub)], axis=0)
        q_b = jnp.concatenate(
            [zsub if i % 2 == 0 else blk(qc, i) * jnp.exp2(blk(bl, i) - row(i * sub - 1)) for i in range(n_sub)],
            axis=0)
        k_b = jnp.concatenate(
            [blk(kc, i) * jnp.exp2(row((i + 1) * sub - 1) - blk(bl, i)) if i % 2 == 0 else zsub
             for i in range(n_sub)], axis=0)
        q_c = jnp.concatenate([zhalf, qc[half:] * jnp.exp2(bl[half:] - row(half - 1))], axis=0)
        k_c = jnp.concatenate([kc[:half] * jnp.exp2(row(half - 1) - bl[:half]), zhalf], axis=0)
        s_a = _dot_nt(stack_heads(q_a), k_a.astype(BF16))
        s_b = _dot_nt(stack_heads(q_b), k_b.astype(BF16))
        s_c = _dot_nt(stack_heads(q_c), k_c.astype(BF16))
        p = jnp.where(m_same32, s_a, jnp.where(m_same64, s_b, s_c)).astype(BF16)
        pcat = jnp.concatenate([p[hd * HG_CHUNK:(hd + 1) * HG_CHUNK] for hd in range(HG_HEADS)], axis=1)
        o_intra = _dot(pcat, stack_heads(vc))
        b_end = row(HG_CHUNK - 1)
        qs = (qc * jnp.exp2(bl)).astype(BF16)
        ke = (kc * jnp.exp2(b_end - bl)).astype(BF16)
        o_chunk = o_intra + _dot_nt(qs, st.astype(BF16))
        return o_chunk, st * jnp.exp2(b_end) + jnp.where(emask, _dot_tn(vc.astype(BF16), ke), 0.0)

    def pool_group():
        pv = proj[:, OFF_PV:OFF_PV + POOL_WIDTH]
        ptail_ref[POOL_TAIL:POOL_TAIL + tile, :] = pv
        lane_p = lax.broadcasted_iota(jnp.int32, (1, POOL_WIDTH), 1)
        win_l = jnp.left_shift(2, lane_p // POOL_GDIM)
        groups_per_col = 128 // POOL_GDIM
        first_group = lax.broadcasted_iota(jnp.int32, (tile, 128), 1) < POOL_GDIM
        cols = []
        for col in range(POOL_WIDTH // 128):
            cur = ptail_ref[:, col * 128:(col + 1) * 128]
            start = 0
            sums = []
            for level in range((col + 1) * groups_per_col):
                shift = 1 << level
                rows = cur.shape[0]
                cur = cur[SUBLANES:] + cur[SUBLANES - shift:rows - shift]
                start += SUBLANES
                sums.append(cur[POOL_TAIL - start:])
            cols.append(jnp.where(first_group, sums[-2], sums[-1]))
        acc_p = jnp.concatenate(cols, axis=1)
        t_glob = s_idx * tile + lax.broadcasted_iota(jnp.int32, (tile, POOL_WIDTH), 0)
        cnt = jnp.minimum(t_glob + 1, win_l).astype(F32)
        pooled = acc_p / cnt - pv
        ptail_ref[0:POOL_TAIL, :] = ptail_ref[pl.ds(tile, POOL_TAIL), :]
        return _dot(pooled.astype(BF16), poolw_ref[...]) * pools_ref[...]

    o_b = pool_group()

    group = ATT_HEADS // ATT_KV_HEADS
    pair_rows = (group // 2) * ATT_BLOCK
    low_half = lax.broadcasted_iota(jnp.int32, (ATT_BLOCK, ATT_KV_WIDTH), 1) < ATT_HDIM
    qi = lax.broadcasted_iota(jnp.int32, (pair_rows, ATT_BLOCK), 0) % ATT_BLOCK
    kj = lax.broadcasted_iota(jnp.int32, (pair_rows, ATT_BLOCK), 1)
    upper = kj > qi
    first_pair = lax.broadcasted_iota(jnp.int32, (pair_rows, 1), 0) < ATT_BLOCK
    q_scale = (ATT_HDIM ** -0.5) * LOG2E
    def att_block(a, prev):
        r = slice(a * ATT_BLOCK, (a + 1) * ATT_BLOCK)
        cur_k = _kv_variants(proj[r, OFF_AK:OFF_AK + ATT_KV_WIDTH], low_half)
        cur_v = _kv_variants(proj[r, OFF_AV:OFF_AV + ATT_KV_WIDTH], low_half)
        if prev is None:
            prev_k = tuple(kv_ref[n] for n in range(4))
            prev_v = tuple(kv_ref[4 + n] for n in range(4))
            no_prev = jnp.where(upper & (s_idx == 0), -jnp.inf, 0.0)
        else:
            prev_k, prev_v = prev
            no_prev = None
        pair_out = []
        for g in range(ATT_KV_HEADS):
            c0 = OFF_AQ + g * group * ATT_HDIM
            q2 = jnp.concatenate(
                [(proj[r, c0 + p * 2 * ATT_HDIM:c0 + (p + 1) * 2 * ATT_HDIM] * q_scale).astype(BF16)
                 for p in range(group // 2)], axis=0)
            kcat = jnp.concatenate([prev_k[2 * g], cur_k[2 * g], prev_k[2 * g + 1], cur_k[2 * g + 1]], axis=0)
            s_all = _dot_nt(q2, kcat)
            o_g = None
            for half in range(2):
                s_prev = s_all[:, (2 * half) * ATT_BLOCK:(2 * half + 1) * ATT_BLOCK]
                s_cur = s_all[:, (2 * half + 1) * ATT_BLOCK:(2 * half + 2) * ATT_BLOCK]
                if no_prev is not None:
                    s_prev = s_prev + no_prev
                s = jnp.where(upper, s_prev, s_cur)
                head0 = g * group + half
                sink = jnp.where(first_pair, sink_ref[head0], sink_ref[head0 + 2]) * LOG2E
                m = jnp.maximum(jnp.max(s, axis=-1, keepdims=True), sink)
                pexp = jnp.exp2(s - m)
                denom = jnp.sum(pexp, axis=-1, keepdims=True) + jnp.exp2(sink - m)
                pcat = jnp.concatenate([jnp.where(upper, pexp, 0.0).astype(BF16),
                                        jnp.where(upper, 0.0, pexp).astype(BF16)], axis=1)
                vcat = jnp.concatenate([prev_v[2 * g + half], cur_v[2 * g + half]], axis=0)
                o_h = _dot(pcat, vcat) * (1.0 / denom)
                o_g = o_h if o_g is None else o_g + o_h
            for p in range(group // 2):
                pair_out.append(o_g[p * ATT_BLOCK:(p + 1) * ATT_BLOCK])
        return jnp.concatenate(pair_out, axis=1), (cur_k, cur_v)

    assert HG_CHUNK == ATT_BLOCK
    st = st_ref[...]
    prev = None
    oa_chunks, o_c_blocks = [], []
    for j in range(tile // ATT_BLOCK):
        o_chunk, st = hg_chunk(j, st)
        oa_chunks.append(o_chunk)
        o_block, prev = att_block(j, prev)
        o_c_blocks.append(o_block)
    oa_fast = jnp.concatenate(oa_chunks, axis=0)
    st_fast = st
    for n in range(4):
        kv_ref[n] = prev[0][n]
        kv_ref[4 + n] = prev[1][n]
    o_c = jnp.concatenate(o_c_blocks, axis=0)

    mix_bc = _dot(jnp.concatenate([o_b.astype(BF16), o_c.astype(BF16)], axis=1), wout_ref[0, HG_WIDTH:, :])

    oa_s[...] = oa_fast

    @pl.when(factorisable)
    def _():
        st_ref[...] = st_fast

    pl.when(jnp.logical_not(factorisable))(pairwise_path)

    oa = oa_s[...]
    ms_h = _dot((oa * oa).astype(BF16), eblk) * (1.0 / HG_DV)
    o_a = oa * lax.rsqrt(ms_h + EPS) * out_gain

    mix = mix_bc + _dot(o_a.astype(BF16), wout_ref[0, 0:HG_WIDTH, :])
    msm = jnp.mean(mix * mix, axis=-1, keepdims=True)
    o_ref[0] = x_ref[0] + (mix * lax.rsqrt(msm + EPS)) * (gt1 * gpost_ref[...])


def _const_spec(shape):
    nd = len(shape)
    return pl.BlockSpec(shape, lambda *_, _nd=nd: (0,) * _nd, pipeline_mode=pl.Buffered(1))


def _layer_spec(shape, layer):
    nd = len(shape)
    return pl.BlockSpec((1,) + tuple(shape[1:]), lambda *_, _nd=nd, _l=layer: (_l,) + (0,) * (_nd - 1),
                        pipeline_mode=pl.Buffered(1))


def _mix_call(layer, x, mod, g_pre, g_post, w_in, lbp, hg_norm_t, pool_bd, pool_scale, sinks, w_out, eblk, cum):
    bsz, seq, d = x.shape
    tile = min(MIX_TILE, seq)
    grid = (bsz, seq // tile)
    row = lambda v: v.reshape(1, -1)
    in_specs = [
        pl.BlockSpec((1, tile, d), lambda b, s: (b, s, 0)),
        pl.BlockSpec((1, 6, d), lambda b, s: (b, 0, 0)),
        _const_spec((1, d)), _const_spec((1, d)),
        _layer_spec(w_in.shape, layer),
        _layer_spec(lbp.shape, layer),
        _const_spec((1, HG_WIDTH)),
        _const_spec(pool_bd.shape),
        _const_spec((1, POOL_WIDTH)),
        pl.BlockSpec(memory_space=pltpu.SMEM),
        _layer_spec(w_out.shape, layer),
        _const_spec(eblk.shape),
        _const_spec(cum.shape),
    ]
    scratch = [
        pltpu.VMEM((1,) + tuple(w_in.shape[1:]), BF16),
        pltpu.VMEM((1,) + tuple(w_out.shape[1:]), BF16),
        pltpu.VMEM((HG_WIDTH, HG_WIDTH), F32),
        pltpu.VMEM((8, ATT_BLOCK, ATT_KV_WIDTH), BF16),
        pltpu.VMEM((POOL_TAIL + tile, POOL_WIDTH), F32),
        pltpu.VMEM((tile, HG_WIDTH), F32),
        pltpu.VMEM((tile, HG_WIDTH), F32),
        pltpu.VMEM((tile, HG_WIDTH), F32),
        pltpu.VMEM((tile, HG_WIDTH), F32),
        pltpu.VMEM((tile, HG_WIDTH), F32),
    ]
    return pl.pallas_call(
        _mix_kernel,
        grid=grid,
        in_specs=in_specs,
        out_specs=pl.BlockSpec((1, tile, d), lambda b, s: (b, s, 0)),
        out_shape=jax.ShapeDtypeStruct(x.shape, F32),
        scratch_shapes=scratch,
        compiler_params=pltpu.CompilerParams(
            dimension_semantics=("arbitrary", "arbitrary"), vmem_limit_bytes=VMEM_LIMIT),
        name="token_mixer",
    )(x, mod, row(g_pre), row(g_post), w_in, lbp, row(hg_norm_t), pool_bd, row(pool_scale), sinks,
      w_out, eblk, cum)


def _gelu_tanh(x):
    c = 0.7978845608028654
    return (0.5 * x) * (1.0 + jnp.tanh(x * (c + (c * 0.044715) * (x * x))))


def _load_weights_bf16(src_hbm, layer, dst_ref, stage_ref, sem_ref, axis):
    step = stage_ref.shape[1 + axis]
    n = src_hbm.shape[1 + axis] // step

    def window(ref, j):
        return ref.at[:, pl.ds(j * step, step)] if axis == 1 else ref.at[pl.ds(j * step, step), :]

    def copy(j):
        return pltpu.make_async_copy(window(src_hbm.at[layer], j), stage_ref.at[j % 2], sem_ref.at[j % 2])

    copy(0).start()
    for j in range(n):
        if j + 1 < n:
            copy(j + 1).start()
        copy(j).wait()
        if axis == 1:
            dst_ref[0, :, j * step:(j + 1) * step] = stage_ref[j % 2].astype(BF16)
        else:
            dst_ref[0, j * step:(j + 1) * step, :] = stage_ref[j % 2].astype(BF16)


def _ffn_kernel(layer, x_ref, mod_ref, gpre_ref, gpost_ref, wup_hbm, cw_ref, cb_ref, wdown_hbm,
                o_ref,
                wup_ref, wdown_ref, stage_up, stage_down, sem_up, sem_down, tail_ref, ubuf_ref, act_ref):
    s_idx = pl.program_id(1)
    tile = x_ref.shape[1]
    d_model = x_ref.shape[2]

    @pl.when((pl.program_id(0) == 0) & (s_idx == 0))
    def _():
        _load_weights_bf16(wup_hbm, layer, wup_ref, stage_up, sem_up, axis=1)
        _load_weights_bf16(wdown_hbm, layer, wdown_ref, stage_down, sem_down, axis=0)

    @pl.when(s_idx == 0)
    def _():
        tail_ref[...] = jnp.zeros_like(tail_ref)

    x = x_ref[0]
    ms = jnp.mean(x * x, axis=-1, keepdims=True)
    gain = gpre_ref[...] * (1.0 + mod_ref[0, 4:5, :])
    hb = ((x * lax.rsqrt(ms + EPS)) * gain + mod_ref[0, 3:4, :]).astype(BF16)

    def up_cols(off, slot):
        ubuf = ubuf_ref.at[slot]
        u = _dot(hb, wup_ref[0, :, off:off + FFN_CHUNK])
        ubuf[0:SUBLANES, :] = tail_ref[:, off:off + FFN_CHUNK]
        ubuf[SUBLANES:SUBLANES + tile, :] = u
        tail_ref[:, off:off + FFN_CHUNK] = ubuf[pl.ds(tile, SUBLANES), :]
        return u

    def conv_cols(u, off, slot):
        ubuf = ubuf_ref.at[slot]
        w0 = cw_ref[0:1, off:off + FFN_CHUNK]
        w1 = cw_ref[1:2, off:off + FFN_CHUNK]
        w2 = cw_ref[2:3, off:off + FFN_CHUNK]
        return (w0 * ubuf[pl.ds(SUBLANES - 2, tile), :] + w1 * ubuf[pl.ds(SUBLANES - 1, tile), :]
                + w2 * u + cb_ref[0:1, off:off + FFN_CHUNK])

    n_chunks = D_FF // FFN_CHUNK
    pending = (up_cols(0, 0), up_cols(D_FF, 1))
    for c in range(n_chunks):
        u_gate, u_val = pending
        if c + 1 < n_chunks:
            nxt = 2 * ((c + 1) % 2)
            pending = (up_cols((c + 1) * FFN_CHUNK, nxt), up_cols(D_FF + (c + 1) * FFN_CHUNK, nxt + 1))
        if c == n_chunks - FFN_EARLY_DOWN:
            k_early = c * FFN_CHUNK
            ys_early = [_dot(act_ref[:, 0:k_early], wdown_ref[0, 0:k_early, n:n + FFN_CHUNK])
                        for n in range(0, d_model, FFN_CHUNK)]
        slot = 2 * (c % 2)
        g = conv_cols(u_gate, c * FFN_CHUNK, slot)
        v = conv_cols(u_val, D_FF + c * FFN_CHUNK, slot + 1)
        act_ref[:, c * FFN_CHUNK:(c + 1) * FFN_CHUNK] = (_gelu_tanh(g) * v).astype(BF16)

    ys = [y_early + _dot(act_ref[:, k_early:], wdown_ref[0, k_early:, n:n + FFN_CHUNK])
          for y_early, n in zip(ys_early, range(0, d_model, FFN_CHUNK))]
    ssq = None
    for y in ys:
        part = jnp.sum(y * y, axis=-1, keepdims=True)
        ssq = part if ssq is None else ssq + part
    rs = lax.rsqrt(ssq * (1.0 / d_model) + EPS)
    out_gain = mod_ref[0, 5:6, :] * gpost_ref[...]
    for g, y in enumerate(ys):
        cols = slice(g * FFN_CHUNK, (g + 1) * FFN_CHUNK)
        o_ref[0, :, cols] = x_ref[0, :, cols] + (y * rs) * out_gain[:, cols]


def _ffn_call(layer, x, mod, g_pre, g_post, w_up, conv_w, conv_b, w_down):
    bsz, seq, d = x.shape
    tile = min(FFN_TILE, seq)
    grid = (bsz, seq // tile)
    row = lambda v: v.reshape(1, -1)
    in_specs = [
        pl.BlockSpec((1, tile, d), lambda b, s: (b, s, 0)),
        pl.BlockSpec((1, 6, d), lambda b, s: (b, 0, 0)),
        _const_spec((1, d)), _const_spec((1, d)),
        pl.BlockSpec(memory_space=pl.ANY),
        _const_spec(conv_w.shape),
        _const_spec((1, 2 * D_FF)),
        pl.BlockSpec(memory_space=pl.ANY),
    ]
    scratch = [
        pltpu.VMEM((1,) + tuple(w_up.shape[1:]), BF16),
        pltpu.VMEM((1,) + tuple(w_down.shape[1:]), BF16),
        pltpu.VMEM((2, w_up.shape[1], FFN_STAGE), F32),
        pltpu.VMEM((2, FFN_STAGE // 2, w_down.shape[2]), F32),
        pltpu.SemaphoreType.DMA((2,)),
        pltpu.SemaphoreType.DMA((2,)),
        pltpu.VMEM((SUBLANES, 2 * D_FF), F32),
        pltpu.VMEM((4, SUBLANES + tile, FFN_CHUNK), F32),
        pltpu.VMEM((tile, D_FF), BF16),
    ]
    return pl.pallas_call(
        functools.partial(_ffn_kernel, layer),
        grid=grid,
        in_specs=in_specs,
        out_specs=pl.BlockSpec((1, tile, d), lambda b, s: (b, s, 0)),
        out_shape=jax.ShapeDtypeStruct(x.shape, F32),
        scratch_shapes=scratch,
        compiler_params=pltpu.CompilerParams(
            dimension_semantics=("arbitrary", "arbitrary"), vmem_limit_bytes=VMEM_LIMIT),
        name="conv_ffn",
    )(x, mod, row(g_pre), row(g_post), w_up, conv_w, row(conv_b), w_down)


def _block_diag(blocks):
    g, c, _ = blocks.shape
    eye = jnp.eye(g, dtype=blocks.dtype)
    return (eye[:, None, :, None] * blocks[:, :, None, :]).reshape(g * c, g * c)


def _cum_matrix(rows):
    r = jnp.arange(rows)
    return (r[None, :] <= r[:, None]).astype(BF16)


def kernel(x, c, w_ada, b_ada, g_pre_mix, g_post_mix, w_in, hg_lb_logits, hg_norm, pool_w, pool_scale,
           attn_sinks, w_out, g_pre_ffn, g_post_ffn, w_up, conv_w, conv_b, w_down):
    depth = w_ada.shape[0]
    bsz, seq, d = x.shape
    mod_all, lbp_all = _ada_call(c, w_ada, b_ada, hg_lb_logits)
    mod_all = mod_all.reshape(depth, bsz, 6, d)
    eblk = _block_diag(jnp.ones((HG_HEADS, HG_DK, HG_DV), BF16))
    cum = _cum_matrix(HG_CHUNK)
    for l in range(depth):
        x = _mix_call(
            l, x, mod_all[l], g_pre_mix[l], g_post_mix[l], w_in, lbp_all,
            jnp.tile(hg_norm[l], HG_HEADS), _block_diag(pool_w[l]).astype(BF16), pool_scale[l],
            attn_sinks[l], w_out, eblk, cum)
        x = _ffn_call(
            l, x, mod_all[l], g_pre_ffn[l], g_post_ffn[l], w_up, conv_w[l], conv_b[l], w_down)
    return x
```

```python
import functools

import jax
import jax.numpy as jnp
from jax import lax
from jax.experimental import pallas as pl
from jax.experimental.pallas import tpu as pltpu

F32 = jnp.float32
BF16 = jnp.bfloat16

D_MODEL = 1024
HG_HEADS = 4
HG_DK = 64
HG_DV = 64
HG_WIDTH = HG_HEADS * HG_DV
POOL_WINDOWS = (2, 4, 8, 16)
POOL_GDIM = 64
POOL_WIDTH = len(POOL_WINDOWS) * POOL_GDIM
POOL_TAIL = 32
ATT_HEADS = 8
ATT_KV_HEADS = 2
ATT_HDIM = 64
ATT_WIDTH = ATT_HEADS * ATT_HDIM
ATT_KV_WIDTH = ATT_KV_HEADS * ATT_HDIM
WINDOW = 128
ATT_BLOCK = WINDOW
D_FF = 2816
CONV_WIDTH = 3
EPS = 1e-6
LOG2E = 1.4426950408889634
D_IN = 4 * HG_WIDTH + POOL_WIDTH + ATT_WIDTH + 2 * ATT_KV_WIDTH
OFF_HQ = 0
OFF_HF = OFF_HQ + HG_WIDTH
OFF_HI = OFF_HF + HG_WIDTH
OFF_HG = OFF_HI + HG_WIDTH
OFF_PV = OFF_HG + HG_WIDTH
OFF_AQ = OFF_PV + POOL_WIDTH
OFF_AK = OFF_AQ + ATT_WIDTH
OFF_AV = OFF_AK + ATT_KV_WIDTH

HG_CHUNK = 128
HG_SUB = 32
HG_BLOCK = 16
HG_SAFE_EXP = 60.0
MIX_TILE = 512
PROJ_GROUP = 1024
FFN_TILE = 512
FFN_CHUNK = 256
FFN_STAGE = 512
FFN_STAGE_SLOTS = 4
FFN_EARLY_DOWN = 2
ADA_TILE = 3072
SUBLANES = 8
VMEM_LIMIT = 56 * 1024 * 1024


def _sigmoid(x):
    return 1.0 / (1.0 + jnp.exp(-x))


def _split3(x):
    hi = x.astype(BF16)
    r1 = x - hi.astype(F32)
    mid = r1.astype(BF16)
    lo = (r1 - mid.astype(F32)).astype(BF16)
    return hi, mid, lo


def _dot(a, b):
    return jnp.dot(a, b, preferred_element_type=F32)


def _dot_nt(a, b):
    return lax.dot_general(a, b, (((1,), (1,)), ((), ())), preferred_element_type=F32)


def _dot_tn(a, b):
    return lax.dot_general(a, b, (((0,), (0,)), ((), ())), preferred_element_type=F32)


def _ada_kernel(c_ref, w_ref, b_ref, lbl_ref, mod_ref, lbp_ref):
    c = c_ref[...]
    ca = c * _sigmoid(c)
    acc = None
    wb = w_ref[0].astype(BF16)
    for piece in _split3(ca):
        t = _dot(piece, wb)
        acc = t if acc is None else acc + t
    mod_ref[0] = acc + b_ref[0]

    logits = lbl_ref[...]
    depth = logits.shape[0]
    mx = jnp.max(logits, axis=0, keepdims=True)
    ex = jnp.exp(logits - mx)
    sm = ex / jnp.sum(ex, axis=0, keepdims=True)
    run = sm[0:1]
    first = run
    for l in range(depth):
        if l > 0:
            run = run + sm[l:l + 1]
        lb = run - first
        lbp_ref[l, 0:1, :] = jnp.log(lb)
        lbp_ref[l, 1:2, :] = jnp.log1p(-lb)
        lbp_ref[l, 2:3, :] = 1.0 - lb


def _ada_call(c, w_ada, b_ada, hg_lb_logits):
    depth, d, n = w_ada.shape
    bsz = c.shape[0]
    hk = hg_lb_logits.shape[1]
    grid = (depth, n // ADA_TILE)
    return pl.pallas_call(
        _ada_kernel,
        grid=grid,
        in_specs=[
            pl.BlockSpec((bsz, d), lambda l, j: (0, 0)),
            pl.BlockSpec((1, d, ADA_TILE), lambda l, j: (l, 0, j)),
            pl.BlockSpec((1, 1, ADA_TILE), lambda l, j: (l, 0, j)),
            pl.BlockSpec((depth, hk), lambda l, j: (0, 0)),
        ],
        out_specs=[
            pl.BlockSpec((1, bsz, ADA_TILE), lambda l, j: (l, 0, j)),
            pl.BlockSpec((depth, 3, hk), lambda l, j: (0, 0, 0)),
        ],
        out_shape=[
            jax.ShapeDtypeStruct((depth, bsz, n), F32),
            jax.ShapeDtypeStruct((depth, 3, hk), F32),
        ],
        compiler_params=pltpu.CompilerParams(
            dimension_semantics=("arbitrary", "arbitrary"), vmem_limit_bytes=VMEM_LIMIT),
        name="adaln_params",
    )(c, w_ada, b_ada.reshape(depth, 1, n), hg_lb_logits)


def _kv_variants(t, low_half):
    sw = pltpu.roll(t, ATT_HDIM, 1)
    return (jnp.where(low_half, t, 0.0).astype(BF16), jnp.where(low_half, 0.0, sw).astype(BF16),
            jnp.where(low_half, sw, 0.0).astype(BF16), jnp.where(low_half, 0.0, t).astype(BF16))


class _Columns:
    def __init__(self, group_fn, width):
        self.group_fn, self.width, self.pieces = group_fn, width, {}

    def __getitem__(self, idx):
        rows, cols = idx
        piece, lo = divmod(cols.start, self.width)
        assert cols.stop - cols.start <= self.width - lo, "column slice straddles two groups"
        if piece not in self.pieces:
            self.pieces[piece] = self.group_fn(piece * self.width)
        return self.pieces[piece][rows, lo:lo + cols.stop - cols.start]


def _mix_kernel(x_ref, mod_ref, gpre_ref, gpost_ref, win32_ref, lbp_ref, hgn_ref, poolw_ref,
                pools_ref, sink_ref, wout32_ref, eblk_ref, cum_ref,
                o_ref,
                win_ref, wout_ref, st_ref, kv_ref, ptail_ref, qf_s, b_s, kk_s, vi_s, oa_s):
    s_idx = pl.program_id(1)
    tile = x_ref.shape[1]

    @pl.when((pl.program_id(0) == 0) & (s_idx == 0))
    def _():
        win_ref[...] = win32_ref[...].astype(BF16)
        wout_ref[...] = wout32_ref[...].astype(BF16)

    @pl.when(s_idx == 0)
    def _():
        st_ref[...] = jnp.zeros_like(st_ref)
        kv_ref[...] = jnp.zeros_like(kv_ref)
        ptail_ref[0:POOL_TAIL, :] = jnp.zeros((POOL_TAIL, POOL_WIDTH), F32)

    x = x_ref[0]
    gt1 = mod_ref[0, 2:3, :]
    ms = jnp.mean(x * x, axis=-1, keepdims=True)
    hb = ((x * lax.rsqrt(ms + EPS)) * (gpre_ref[...] * (1.0 + mod_ref[0, 1:2, :])) + mod_ref[0, 0:1, :]).astype(BF16)
    proj = _Columns(lambda c: _dot(hb, win_ref[0, :, c:c + PROJ_GROUP]), PROJ_GROUP)

    eblk = eblk_ref[...]
    emask = eblk > 0

    q = proj[:, OFF_HQ:OFF_HQ + HG_WIDTH]
    z = proj[:, OFF_HF:OFF_HF + HG_WIDTH]
    qf = q * _sigmoid(q) * (HG_DK ** -0.5)
    log_sig = jnp.minimum(z, 0.0) - jnp.log(1.0 + jnp.exp(-jnp.abs(z)))
    la = lbp_ref[0, 0:1, :]
    lc = lbp_ref[0, 1:2, :] + log_sig
    log_f = jnp.maximum(la, lc) + jnp.log(1.0 + jnp.exp(-jnp.abs(la - lc)))
    kk = lbp_ref[0, 2:3, :] * _sigmoid(-z)
    vi = proj[:, OFF_HI:OFF_HI + HG_WIDTH]
    gate = proj[:, OFF_HG:OFF_HG + HG_WIDTH]
    out_gain = hgn_ref[...] * (gate * _sigmoid(gate))
    parts = jnp.concatenate(_split3(log_f), axis=1)
    cums = jnp.concatenate(
        [_dot(cum_ref[...], parts[c:c + HG_CHUNK]) for c in range(0, tile, HG_CHUNK)], axis=0)
    w = HG_WIDTH
    b_loc = cums[:, 0:w] + cums[:, w:2 * w] + cums[:, 2 * w:3 * w]

    sub = HG_SUB
    spread = None
    for i in range(tile // sub):
        bl = b_loc[i * sub:(i + 1) * sub]
        sp = jnp.max(jnp.abs(bl - bl[sub // 2 - 1:sub // 2]))
        spread = sp if spread is None else jnp.maximum(spread, sp)
    factorisable = spread <= HG_SAFE_EXP

    def pairwise_path():
        qf_s[...] = qf
        b_s[...] = b_loc
        kk_s[...] = kk
        vi_s[...] = vi
        rows16 = lax.broadcasted_iota(jnp.int32, (HG_BLOCK, HG_WIDTH), 0)
        per_chunk = HG_CHUNK // HG_BLOCK

        def hg_step(i, carry):
            r0 = pl.multiple_of(i * HG_BLOCK, HG_BLOCK)
            qb = qf_s[pl.ds(r0, HG_BLOCK), :]
            bb = b_s[pl.ds(r0, HG_BLOCK), :]
            kb = kk_s[pl.ds(r0, HG_BLOCK), :]
            b_prev = jnp.where(i % per_chunk == 0, 0.0, b_s[pl.ds(jnp.maximum(r0 - 1, 0), 1), :])
            b_end = b_s[pl.ds(r0 + HG_BLOCK - 1, 1), :]
            pieces = []
            for j in range(HG_BLOCK):
                bj = b_s[pl.ds(r0 + j, 1), :]
                kj_row = kk_s[pl.ds(r0 + j, 1), :]
                d = jnp.where(rows16 >= j, (qb * kj_row) * jnp.exp(bb - bj), 0.0)
                pieces.append(d.astype(BF16))
            dall = jnp.concatenate(pieces, axis=0)
            sall = _dot(dall, eblk)
            acc = jnp.zeros((HG_BLOCK, HG_WIDTH), F32)
            for j in range(HG_BLOCK):
                vj = vi_s[pl.ds(r0 + j, 1), :]
                acc = acc + sall[j * HG_BLOCK:(j + 1) * HG_BLOCK, :] * vj
            st_old = st_ref[...]
            qs_b = (qb * jnp.exp(bb - b_prev)).astype(BF16)
            ke_b = (kb * jnp.exp(b_end - bb)).astype(BF16)
            oa_s[pl.ds(r0, HG_BLOCK), :] = acc + _dot_nt(qs_b, st_old.astype(BF16))
            vb_b = vi_s[pl.ds(r0, HG_BLOCK), :].astype(BF16)
            st_ref[...] = st_old * jnp.exp(b_end - b_prev) + jnp.where(emask, _dot_tn(vb_b, ke_b), 0.0)
            return carry

        lax.fori_loop(0, tile // HG_BLOCK, hg_step, 0)

    b2 = b_loc * LOG2E
    n_sub = HG_CHUNK // sub
    half = HG_CHUNK // 2
    lane_head = lax.broadcasted_iota(jnp.int32, (HG_CHUNK, HG_WIDTH), 1) // HG_DK
    head_keep = [jnp.where(lane_head == hd, 1.0, 0.0).astype(BF16) for hd in range(HG_HEADS)]

    def stack_heads(a):
        ab = a.astype(BF16)
        return jnp.concatenate([ab * keep for keep in head_keep], axis=0)

    ti = lax.broadcasted_iota(jnp.int32, (HG_HEADS * HG_CHUNK, HG_CHUNK), 0) % HG_CHUNK
    tj = lax.broadcasted_iota(jnp.int32, (HG_HEADS * HG_CHUNK, HG_CHUNK), 1)
    m_same32 = (ti // sub == tj // sub) & (ti >= tj)
    m_same64 = (ti // (2 * sub) == tj // (2 * sub)) & (ti // sub > tj // sub)
    zsub = jnp.zeros((sub, HG_WIDTH), F32)
    zhalf = jnp.zeros((half, HG_WIDTH), F32)
    def hg_chunk(c, st):
        r = slice(c * HG_CHUNK, (c + 1) * HG_CHUNK)
        bl = b2[r]
        qc = qf[r]
        kc = kk[r]
        vc = vi[r]
        blk = lambda a, i: a[i * sub:(i + 1) * sub]
        row = lambda i: bl[i:i + 1]
        mids = [row(i * sub + sub // 2 - 1) for i in range(n_sub)]
        q_a = jnp.concatenate([blk(qc, i) * jnp.exp2(blk(bl, i) - mids[i]) for i in range(n_sub)], axis=0)
        k_a = jnp.concatenate([blk(kc, i) * jnp.exp2(mids[i] - blk(bl, i)) for i in range(n_sub)], axis=0)
        q_b = jnp.concatenate(
            [zsub if i % 2 == 0 else blk(qc, i) * jnp.exp2(blk(bl, i) - row(i * sub - 1)) for i in range(n_sub)],
            axis=0)
        k_b = jnp.concatenate(
            [blk(kc, i) * jnp.exp2(row((i + 1) * sub - 1) - blk(bl, i)) if i % 2 == 0 else zsub
             for i in range(n_sub)], axis=0)
        q_c = jnp.concatenate([zhalf, qc[half:] * jnp.exp2(bl[half:] - row(half - 1))], axis=0)
        k_c = jnp.concatenate([kc[:half] * jnp.exp2(row(half - 1) - bl[:half]), zhalf], axis=0)
        s_a = _dot_nt(stack_heads(q_a), k_a.astype(BF16))
        s_b = _dot_nt(stack_heads(q_b), k_b.astype(BF16))
        s_c = _dot_nt(stack_heads(q_c), k_c.astype(BF16))
        p = jnp.where(m_same32, s_a, jnp.where(m_same64, s_b, s_c)).astype(BF16)
        pcat = jnp.concatenate([p[hd * HG_CHUNK:(hd + 1) * HG_CHUNK] for hd in range(HG_HEADS)], axis=1)
        o_intra = _dot(pcat, stack_heads(vc))
        b_end = row(HG_CHUNK - 1)
        qs = (qc * jnp.exp2(bl)).astype(BF16)
        ke = (kc * jnp.exp2(b_end - bl)).astype(BF16)
        o_chunk = o_intra + _dot_nt(qs, st.astype(BF16))
        return o_chunk, st * jnp.exp2(b_end) + jnp.where(emask, _dot_tn(vc.astype(BF16), ke), 0.0)

    def pool_group():
        pv = proj[:, OFF_PV:OFF_PV + POOL_WIDTH]
        ptail_ref[POOL_TAIL:POOL_TAIL + tile, :] = pv
        lane_p = lax.broadcasted_iota(jnp.int32, (1, POOL_WIDTH), 1)
        win_l = jnp.left_shift(2, lane_p // POOL_GDIM)
        groups_per_col = 128 // POOL_GDIM
        first_group = lax.broadcasted_iota(jnp.int32, (tile, 128), 1) < POOL_GDIM
        cols = []
        for col in range(POOL_WIDTH // 128):
            cur = ptail_ref[:, col * 128:(col + 1) * 128]
            start = 0
            sums = []
            for level in range((col + 1) * groups_per_col):
                shift = 1 << level
                rows = cur.shape[0]
                cur = cur[SUBLANES:] + cur[SUBLANES - shift:rows - shift]
                start += SUBLANES
                sums.append(cur[POOL_TAIL - start:])
            cols.append(jnp.where(first_group, sums[-2], sums[-1]))
        acc_p = jnp.concatenate(cols, axis=1)
        t_glob = s_idx * tile + lax.broadcasted_iota(jnp.int32, (tile, POOL_WIDTH), 0)
        cnt = jnp.minimum(t_glob + 1, win_l).astype(F32)
        pooled = acc_p / cnt - pv
        ptail_ref[0:POOL_TAIL, :] = ptail_ref[pl.ds(tile, POOL_TAIL), :]
        return _dot(pooled.astype(BF16), poolw_ref[...]) * pools_ref[...]

    o_b = pool_group()

    group = ATT_HEADS // ATT_KV_HEADS
    pair_rows = (group // 2) * ATT_BLOCK
    low_half = lax.broadcasted_iota(jnp.int32, (ATT_BLOCK, ATT_KV_WIDTH), 1) < ATT_HDIM
    qi = lax.broadcasted_iota(jnp.int32, (pair_rows, ATT_BLOCK), 0) % ATT_BLOCK
    kj = lax.broadcasted_iota(jnp.int32, (pair_rows, ATT_BLOCK), 1)
    upper = kj > qi
    first_pair = lax.broadcasted_iota(jnp.int32, (pair_rows, 1), 0) < ATT_BLOCK
    q_scale = (ATT_HDIM ** -0.5) * LOG2E
    def att_block(a, prev):
        r = slice(a * ATT_BLOCK, (a + 1) * ATT_BLOCK)
        cur_k = _kv_variants(proj[r, OFF_AK:OFF_AK + ATT_KV_WIDTH], low_half)
        cur_v = _kv_variants(proj[r, OFF_AV:OFF_AV + ATT_KV_WIDTH], low_half)
        if prev is None:
            prev_k = tuple(kv_ref[n] for n in range(4))
            prev_v = tuple(kv_ref[4 + n] for n in range(4))
            no_prev = jnp.where(upper & (s_idx == 0), -jnp.inf, 0.0)
        else:
            prev_k, prev_v = prev
            no_prev = None
        pair_out = []
        for g in range(ATT_KV_HEADS):
            c0 = OFF_AQ + g * group * ATT_HDIM
            q2 = jnp.concatenate(
                [(proj[r, c0 + p * 2 * ATT_HDIM:c0 + (p + 1) * 2 * ATT_HDIM] * q_scale).astype(BF16)
                 for p in range(group // 2)], axis=0)
            kcat = jnp.concatenate([prev_k[2 * g], cur_k[2 * g], prev_k[2 * g + 1], cur_k[2 * g + 1]], axis=0)
            s_all = _dot_nt(q2, kcat)
            o_g = None
            for half in range(2):
                s_prev = s_all[:, (2 * half) * ATT_BLOCK:(2 * half + 1) * ATT_BLOCK]
                s_cur = s_all[:, (2 * half + 1) * ATT_BLOCK:(2 * half + 2) * ATT_BLOCK]
                if no_prev is not None:
                    s_prev = s_prev + no_prev
                s = jnp.where(upper, s_prev, s_cur)
                head0 = g * group + half
                sink = jnp.where(first_pair, sink_ref[head0], sink_ref[head0 + 2]) * LOG2E
                m = jnp.maximum(jnp.max(s, axis=-1, keepdims=True), sink)
                pexp = jnp.exp2(s - m)
                denom = jnp.sum(pexp, axis=-1, keepdims=True) + jnp.exp2(sink - m)
                pcat = jnp.concatenate([jnp.where(upper, pexp, 0.0).astype(BF16),
                                        jnp.where(upper, 0.0, pexp).astype(BF16)], axis=1)
                vcat = jnp.concatenate([prev_v[2 * g + half], cur_v[2 * g + half]], axis=0)
                o_h = _dot(pcat, vcat) * (1.0 / denom)
                o_g = o_h if o_g is None else o_g + o_h
            for p in range(group // 2):
                pair_out.append(o_g[p * ATT_BLOCK:(p + 1) * ATT_BLOCK])
        return jnp.concatenate(pair_out, axis=1), (cur_k, cur_v)

    assert HG_CHUNK == ATT_BLOCK
    st = st_ref[...]
    prev = None
    oa_chunks, o_c_blocks = [], []
    for j in range(tile // ATT_BLOCK):
        o_chunk, st = hg_chunk(j, st)
        oa_chunks.append(o_chunk)
        o_block, prev = att_block(j, prev)
        o_c_blocks.append(o_block)
    oa_fast = jnp.concatenate(oa_chunks, axis=0)
    st_fast = st
    for n in range(4):
        kv_ref[n] = prev[0][n]
        kv_ref[4 + n] = prev[1][n]
    o_c = jnp.concatenate(o_c_blocks, axis=0)

    mix_bc = _dot(jnp.concatenate([o_b.astype(BF16), o_c.astype(BF16)], axis=1), wout_ref[0, HG_WIDTH:, :])

    oa_s[...] = oa_fast

    @pl.when(factorisable)
    def _():
        st_ref[...] = st_fast

    pl.when(jnp.logical_not(factorisable))(pairwise_path)

    oa = oa_s[...]
    ms_h = _dot((oa * oa).astype(BF16), eblk) * (1.0 / HG_DV)
    o_a = oa * lax.rsqrt(ms_h + EPS) * out_gain

    mix = mix_bc + _dot(o_a.astype(BF16), wout_ref[0, 0:HG_WIDTH, :])
    msm = jnp.mean(mix * mix, axis=-1, keepdims=True)
    o_ref[0] = x_ref[0] + (mix * lax.rsqrt(msm + EPS)) * (gt1 * gpost_ref[...])


def _const_spec(shape):
    nd = len(shape)
    return pl.BlockSpec(shape, lambda *_, _nd=nd: (0,) * _nd, pipeline_mode=pl.Buffered(1))


def _layer_spec(shape, layer):
    nd = len(shape)
    return pl.BlockSpec((1,) + tuple(shape[1:]), lambda *_, _nd=nd, _l=layer: (_l,) + (0,) * (_nd - 1),
                        pipeline_mode=pl.Buffered(1))


def _mix_call(layer, x, mod, g_pre, g_post, w_in, lbp, hg_norm_t, pool_bd, pool_scale, sinks, w_out, eblk, cum):
    bsz, seq, d = x.shape
    tile = min(MIX_TILE, seq)
    grid = (bsz, seq // tile)
    row = lambda v: v.reshape(1, -1)
    in_specs = [
        pl.BlockSpec((1, tile, d), lambda b, s: (b, s, 0)),
        pl.BlockSpec((1, 6, d), lambda b, s: (b, 0, 0)),
        _const_spec((1, d)), _const_spec((1, d)),
        _layer_spec(w_in.shape, layer),
        _layer_spec(lbp.shape, layer),
        _const_spec((1, HG_WIDTH)),
        _const_spec(pool_bd.shape),
        _const_spec((1, POOL_WIDTH)),
        pl.BlockSpec(memory_space=pltpu.SMEM),
        _layer_spec(w_out.shape, layer),
        _const_spec(eblk.shape),
        _const_spec(cum.shape),
    ]
    scratch = [
        pltpu.VMEM((1,) + tuple(w_in.shape[1:]), BF16),
        pltpu.VMEM((1,) + tuple(w_out.shape[1:]), BF16),
        pltpu.VMEM((HG_WIDTH, HG_WIDTH), F32),
        pltpu.VMEM((8, ATT_BLOCK, ATT_KV_WIDTH), BF16),
        pltpu.VMEM((POOL_TAIL + tile, POOL_WIDTH), F32),
        pltpu.VMEM((tile, HG_WIDTH), F32),
        pltpu.VMEM((tile, HG_WIDTH), F32),
        pltpu.VMEM((tile, HG_WIDTH), F32),
        pltpu.VMEM((tile, HG_WIDTH), F32),
        pltpu.VMEM((tile, HG_WIDTH), F32),
    ]
    return pl.pallas_call(
        _mix_kernel,
        grid=grid,
        in_specs=in_specs,
        out_specs=pl.BlockSpec((1, tile, d), lambda b, s: (b, s, 0)),
        out_shape=jax.ShapeDtypeStruct(x.shape, F32),
        scratch_shapes=scratch,
        compiler_params=pltpu.CompilerParams(
            dimension_semantics=("arbitrary", "arbitrary"), vmem_limit_bytes=VMEM_LIMIT),
        name="token_mixer",
    )(x, mod, row(g_pre), row(g_post), w_in, lbp, row(hg_norm_t), pool_bd, row(pool_scale), sinks,
      w_out, eblk, cum)


def _gelu_tanh(x):
    c = 0.7978845608028654
    return (0.5 * x) * (1.0 + jnp.tanh(x * (c + (c * 0.044715) * (x * x))))


def _load_weights_bf16(src_hbm, layer, dst_ref, stage_ref, sem_ref, axis):
    slots = stage_ref.shape[0]
    step = stage_ref.shape[1 + axis]
    n = src_hbm.shape[1 + axis] // step

    def window(ref, j):
        return ref.at[:, pl.ds(j * step, step)] if axis == 1 else ref.at[pl.ds(j * step, step), :]

    def copy(j):
        return pltpu.make_async_copy(window(src_hbm.at[layer], j), stage_ref.at[j % slots], sem_ref.at[j % slots])

    for j in range(min(slots - 1, n)):
        copy(j).start()
    for j in range(n):
        if j + slots - 1 < n:
            copy(j + slots - 1).start()
        copy(j).wait()
        if axis == 1:
            dst_ref[0, :, j * step:(j + 1) * step] = stage_ref[j % slots].astype(BF16)
        else:
            dst_ref[0, j * step:(j + 1) * step, :] = stage_ref[j % slots].astype(BF16)


def _ffn_kernel(layer, x_ref, mod_ref, gpre_ref, gpost_ref, wup_hbm, cw_ref, cb_ref, wdown_hbm,
                o_ref,
                wup_ref, wdown_ref, stage_up, stage_down, sem_up, sem_down, tail_ref, ubuf_ref, act_ref):
    s_idx = pl.program_id(1)
    tile = x_ref.shape[1]
    d_model = x_ref.shape[2]

    @pl.when((pl.program_id(0) == 0) & (s_idx == 0))
    def _():
        _load_weights_bf16(wup_hbm, layer, wup_ref, stage_up, sem_up, axis=1)
        _load_weights_bf16(wdown_hbm, layer, wdown_ref, stage_down, sem_down, axis=0)

    @pl.when(s_idx == 0)
    def _():
        tail_ref[...] = jnp.zeros_like(tail_ref)

    x = x_ref[0]
    ms = jnp.mean(x * x, axis=-1, keepdims=True)
    gain = gpre_ref[...] * (1.0 + mod_ref[0, 4:5, :])
    hb = ((x * lax.rsqrt(ms + EPS)) * gain + mod_ref[0, 3:4, :]).astype(BF16)

    def up_cols(off, slot):
        ubuf = ubuf_ref.at[slot]
        u = _dot(hb, wup_ref[0, :, off:off + FFN_CHUNK])
        ubuf[0:SUBLANES, :] = tail_ref[:, off:off + FFN_CHUNK]
        ubuf[SUBLANES:SUBLANES + tile, :] = u
        tail_ref[:, off:off + FFN_CHUNK] = ubuf[pl.ds(tile, SUBLANES), :]
        return u

    def conv_cols(u, off, slot):
        ubuf = ubuf_ref.at[slot]
        w0 = cw_ref[0:1, off:off + FFN_CHUNK]
        w1 = cw_ref[1:2, off:off + FFN_CHUNK]
        w2 = cw_ref[2:3, off:off + FFN_CHUNK]
        return (w0 * ubuf[pl.ds(SUBLANES - 2, tile), :] + w1 * ubuf[pl.ds(SUBLANES - 1, tile), :]
                + w2 * u + cb_ref[0:1, off:off + FFN_CHUNK])

    n_chunks = D_FF // FFN_CHUNK
    pending = (up_cols(0, 0), up_cols(D_FF, 1))
    for c in range(n_chunks):
        u_gate, u_val = pending
        if c + 1 < n_chunks:
            nxt = 2 * ((c + 1) % 2)
            pending = (up_cols((c + 1) * FFN_CHUNK, nxt), up_cols(D_FF + (c + 1) * FFN_CHUNK, nxt + 1))
        if c == n_chunks - FFN_EARLY_DOWN:
            k_early = c * FFN_CHUNK
            ys_early = [_dot(act_ref[:, 0:k_early], wdown_ref[0, 0:k_early, n:n + FFN_CHUNK])
                        for n in range(0, d_model, FFN_CHUNK)]
        slot = 2 * (c % 2)
        g = conv_cols(u_gate, c * FFN_CHUNK, slot)
        v = conv_cols(u_val, D_FF + c * FFN_CHUNK, slot + 1)
        act_ref[:, c * FFN_CHUNK:(c + 1) * FFN_CHUNK] = (_gelu_tanh(g) * v).astype(BF16)

    ys = [y_early + _dot(act_ref[:, k_early:], wdown_ref[0, k_early:, n:n + FFN_CHUNK])
          for y_early, n in zip(ys_early, range(0, d_model, FFN_CHUNK))]
    ssq = None
    for y in ys:
        part = jnp.sum(y * y, axis=-1, keepdims=True)
        ssq = part if ssq is None else ssq + part
    rs = lax.rsqrt(ssq * (1.0 / d_model) + EPS)
    out_gain = mod_ref[0, 5:6, :] * gpost_ref[...]
    for g, y in enumerate(ys):
        cols = slice(g * FFN_CHUNK, (g + 1) * FFN_CHUNK)
        o_ref[0, :, cols] = x_ref[0, :, cols] + (y * rs) * out_gain[:, cols]


def _ffn_call(layer, x, mod, g_pre, g_post, w_up, conv_w, conv_b, w_down):
    bsz, seq, d = x.shape
    tile = min(FFN_TILE, seq)
    grid = (bsz, seq // tile)
    row = lambda v: v.reshape(1, -1)
    in_specs = [
        pl.BlockSpec((1, tile, d), lambda b, s: (b, s, 0)),
        pl.BlockSpec((1, 6, d), lambda b, s: (b, 0, 0)),
        _const_spec((1, d)), _const_spec((1, d)),
        pl.BlockSpec(memory_space=pl.ANY),
        _const_spec(conv_w.shape),
        _const_spec((1, 2 * D_FF)),
        pl.BlockSpec(memory_space=pl.ANY),
    ]
    scratch = [
        pltpu.VMEM((1,) + tuple(w_up.shape[1:]), BF16),
        pltpu.VMEM((1,) + tuple(w_down.shape[1:]), BF16),
        pltpu.VMEM((FFN_STAGE_SLOTS, w_up.shape[1], FFN_STAGE), F32),
        pltpu.VMEM((FFN_STAGE_SLOTS, FFN_STAGE // 2, w_down.shape[2]), F32),
        pltpu.SemaphoreType.DMA((FFN_STAGE_SLOTS,)),
        pltpu.SemaphoreType.DMA((FFN_STAGE_SLOTS,)),
        pltpu.VMEM((SUBLANES, 2 * D_FF), F32),
        pltpu.VMEM((4, SUBLANES + tile, FFN_CHUNK), F32),
        pltpu.VMEM((tile, D_FF), BF16),
    ]
    return pl.pallas_call(
        functools.partial(_ffn_kernel, layer),
        grid=grid,
        in_specs=in_specs,
        out_specs=pl.BlockSpec((1, tile, d), lambda b, s: (b, s, 0)),
        out_shape=jax.ShapeDtypeStruct(x.shape, F32),
        scratch_shapes=scratch,
        compiler_params=pltpu.CompilerParams(
            dimension_semantics=("arbitrary", "arbitrary"), vmem_limit_bytes=VMEM_LIMIT),
        name="conv_ffn",
    )(x, mod, row(g_pre), row(g_post), w_up, conv_w, row(conv_b), w_down)


def _block_diag(blocks):
    g, c, _ = blocks.shape
    eye = jnp.eye(g, dtype=blocks.dtype)
    return (eye[:, None, :, None] * blocks[:, :, None, :]).reshape(g * c, g * c)


def _cum_matrix(rows):
    r = jnp.arange(rows)
    return (r[None, :] <= r[:, None]).astype(BF16)


def kernel(x, c, w_ada, b_ada, g_pre_mix, g_post_mix, w_in, hg_lb_logits, hg_norm, pool_w, pool_scale,
           attn_sinks, w_out, g_pre_ffn, g_post_ffn, w_up, conv_w, conv_b, w_down):
    depth = w_ada.shape[0]
    bsz, seq, d = x.shape
    mod_all, lbp_all = _ada_call(c, w_ada, b_ada, hg_lb_logits)
    mod_all = mod_all.reshape(depth, bsz, 6, d)
    eblk = _block_diag(jnp.ones((HG_HEADS, HG_DK, HG_DV), BF16))
    cum = _cum_matrix(HG_CHUNK)
    for l in range(depth):
        x = _mix_call(
            l, x, mod_all[l], g_pre_mix[l], g_post_mix[l], w_in, lbp_all,
            jnp.tile(hg_norm[l], HG_HEADS), _block_diag(pool_w[l]).astype(BF16), pool_scale[l],
            attn_sinks[l], w_out, eblk, cum)
        x = _ffn_call(
            l, x, mod_all[l], g_pre_ffn[l], g_post_ffn[l], w_up, conv_w[l], conv_b[l], w_down)
    return x
```

```python
import functools

import jax
import jax.numpy as jnp
from jax import lax
from jax.experimental import pallas as pl
from jax.experimental.pallas import tpu as pltpu

F32 = jnp.float32
BF16 = jnp.bfloat16

D_MODEL = 1024
HG_HEADS = 4
HG_DK = 64
HG_DV = 64
HG_WIDTH = HG_HEADS * HG_DV
POOL_WINDOWS = (2, 4, 8, 16)
POOL_GDIM = 64
POOL_WIDTH = len(POOL_WINDOWS) * POOL_GDIM
POOL_TAIL = 32
ATT_HEADS = 8
ATT_KV_HEADS = 2
ATT_HDIM = 64
ATT_WIDTH = ATT_HEADS * ATT_HDIM
ATT_KV_WIDTH = ATT_KV_HEADS * ATT_HDIM
WINDOW = 128
ATT_BLOCK = WINDOW
D_FF = 2816
CONV_WIDTH = 3
EPS = 1e-6
LOG2E = 1.4426950408889634
D_IN = 4 * HG_WIDTH + POOL_WIDTH + ATT_WIDTH + 2 * ATT_KV_WIDTH
OFF_HQ = 0
OFF_HF = OFF_HQ + HG_WIDTH
OFF_HI = OFF_HF + HG_WIDTH
OFF_HG = OFF_HI + HG_WIDTH
OFF_PV = OFF_HG + HG_WIDTH
OFF_AQ = OFF_PV + POOL_WIDTH
OFF_AK = OFF_AQ + ATT_WIDTH
OFF_AV = OFF_AK + ATT_KV_WIDTH

HG_CHUNK = 128
HG_SUB = 32
HG_BLOCK = 16
HG_SAFE_EXP = 60.0
MIX_TILE = 512
PROJ_GROUP = 1024
FFN_TILE = 512
FFN_CHUNK = 256
FFN_STAGE = 512
FFN_STAGE_SLOTS = 4
FFN_EARLY_DOWN = 2
ADA_TILE = 3072
SUBLANES = 8
VMEM_LIMIT = 56 * 1024 * 1024


def _sigmoid(x):
    return 1.0 / (1.0 + jnp.exp(-x))


def _split3(x):
    hi = x.astype(BF16)
    r1 = x - hi.astype(F32)
    mid = r1.astype(BF16)
    lo = (r1 - mid.astype(F32)).astype(BF16)
    return hi, mid, lo


def _dot(a, b):
    return jnp.dot(a, b, preferred_element_type=F32)


def _dot_nt(a, b):
    return lax.dot_general(a, b, (((1,), (1,)), ((), ())), preferred_element_type=F32)


def _dot_tn(a, b):
    return lax.dot_general(a, b, (((0,), (0,)), ((), ())), preferred_element_type=F32)


def _ada_kernel(c_ref, w_ref, b_ref, lbl_ref, mod_ref, lbp_ref):
    c = c_ref[...]
    ca = c * _sigmoid(c)
    acc = None
    wb = w_ref[0].astype(BF16)
    for piece in _split3(ca):
        t = _dot(piece, wb)
        acc = t if acc is None else acc + t
    mod_ref[0] = acc + b_ref[0]

    logits = lbl_ref[...]
    depth = logits.shape[0]
    mx = jnp.max(logits, axis=0, keepdims=True)
    ex = jnp.exp(logits - mx)
    sm = ex / jnp.sum(ex, axis=0, keepdims=True)
    run = sm[0:1]
    first = run
    for l in range(depth):
        if l > 0:
            run = run + sm[l:l + 1]
        lb = run - first
        lbp_ref[l, 0:1, :] = jnp.log(lb)
        lbp_ref[l, 1:2, :] = jnp.log1p(-lb)
        lbp_ref[l, 2:3, :] = 1.0 - lb


def _ada_call(c, w_ada, b_ada, hg_lb_logits):
    depth, d, n = w_ada.shape
    bsz = c.shape[0]
    hk = hg_lb_logits.shape[1]
    grid = (depth, n // ADA_TILE)
    return pl.pallas_call(
        _ada_kernel,
        grid=grid,
        in_specs=[
            pl.BlockSpec((bsz, d), lambda l, j: (0, 0)),
            pl.BlockSpec((1, d, ADA_TILE), lambda l, j: (l, 0, j)),
            pl.BlockSpec((1, 1, ADA_TILE), lambda l, j: (l, 0, j)),
            pl.BlockSpec((depth, hk), lambda l, j: (0, 0)),
        ],
        out_specs=[
            pl.BlockSpec((1, bsz, ADA_TILE), lambda l, j: (l, 0, j)),
            pl.BlockSpec((depth, 3, hk), lambda l, j: (0, 0, 0)),
        ],
        out_shape=[
            jax.ShapeDtypeStruct((depth, bsz, n), F32),
            jax.ShapeDtypeStruct((depth, 3, hk), F32),
        ],
        compiler_params=pltpu.CompilerParams(
            dimension_semantics=("arbitrary", "arbitrary"), vmem_limit_bytes=VMEM_LIMIT),
        name="adaln_params",
    )(c, w_ada, b_ada.reshape(depth, 1, n), hg_lb_logits)


def _kv_variants(t, low_half):
    sw = pltpu.roll(t, ATT_HDIM, 1)
    return (jnp.where(low_half, t, 0.0).astype(BF16), jnp.where(low_half, 0.0, sw).astype(BF16),
            jnp.where(low_half, sw, 0.0).astype(BF16), jnp.where(low_half, 0.0, t).astype(BF16))


class _Columns:
    def __init__(self, group_fn, width):
        self.group_fn, self.width, self.pieces = group_fn, width, {}

    def __getitem__(self, idx):
        rows, cols = idx
        piece, lo = divmod(cols.start, self.width)
        assert cols.stop - cols.start <= self.width - lo, "column slice straddles two groups"
        if piece not in self.pieces:
            self.pieces[piece] = self.group_fn(piece * self.width)
        return self.pieces[piece][rows, lo:lo + cols.stop - cols.start]


def _mix_kernel(x_ref, mod_ref, gpre_ref, gpost_ref, win32_ref, lbp_ref, hgn_ref, poolw_ref,
                pools_ref, sink_ref, wout32_ref, eblk_ref, cum_ref,
                o_ref,
                win_ref, wout_ref, st_ref, kv_ref, ptail_ref, qf_s, b_s, kk_s, vi_s, oa_s):
    s_idx = pl.program_id(1)
    tile = x_ref.shape[1]

    @pl.when((pl.program_id(0) == 0) & (s_idx == 0))
    def _():
        win_ref[...] = win32_ref[...].astype(BF16)
        wout_ref[...] = wout32_ref[...].astype(BF16)

    @pl.when(s_idx == 0)
    def _():
        st_ref[...] = jnp.zeros_like(st_ref)
        kv_ref[...] = jnp.zeros_like(kv_ref)
        ptail_ref[0:POOL_TAIL, :] = jnp.zeros((POOL_TAIL, POOL_WIDTH), F32)

    x = x_ref[0]
    gt1 = mod_ref[0, 2:3, :]
    ms = jnp.mean(x * x, axis=-1, keepdims=True)
    hb = ((x * lax.rsqrt(ms + EPS)) * (gpre_ref[...] * (1.0 + mod_ref[0, 1:2, :])) + mod_ref[0, 0:1, :]).astype(BF16)
    proj = _Columns(lambda c: _dot(hb, win_ref[0, :, c:c + PROJ_GROUP]), PROJ_GROUP)

    eblk = eblk_ref[...]
    emask = eblk > 0

    q = proj[:, OFF_HQ:OFF_HQ + HG_WIDTH]
    z = proj[:, OFF_HF:OFF_HF + HG_WIDTH]
    qf = q * _sigmoid(q) * (HG_DK ** -0.5)
    log_sig = jnp.minimum(z, 0.0) - jnp.log(1.0 + jnp.exp(-jnp.abs(z)))
    la = lbp_ref[0, 0:1, :]
    lc = lbp_ref[0, 1:2, :] + log_sig
    log_f = jnp.maximum(la, lc) + jnp.log(1.0 + jnp.exp(-jnp.abs(la - lc)))
    kk = lbp_ref[0, 2:3, :] * _sigmoid(-z)
    vi = proj[:, OFF_HI:OFF_HI + HG_WIDTH]
    gate = proj[:, OFF_HG:OFF_HG + HG_WIDTH]
    out_gain = hgn_ref[...] * (gate * _sigmoid(gate))
    parts = jnp.concatenate(_split3(log_f), axis=1)
    cums = jnp.concatenate(
        [_dot(cum_ref[...], parts[c:c + HG_CHUNK]) for c in range(0, tile, HG_CHUNK)], axis=0)
    w = HG_WIDTH
    b_loc = cums[:, 0:w] + cums[:, w:2 * w] + cums[:, 2 * w:3 * w]

    sub = HG_SUB
    spread = None
    for i in range(tile // sub):
        bl = b_loc[i * sub:(i + 1) * sub]
        sp = jnp.max(jnp.abs(bl - bl[sub // 2 - 1:sub // 2]))
        spread = sp if spread is None else jnp.maximum(spread, sp)
    factorisable = spread <= HG_SAFE_EXP

    def pairwise_path():
        qf_s[...] = qf
        b_s[...] = b_loc
        kk_s[...] = kk
        vi_s[...] = vi
        rows16 = lax.broadcasted_iota(jnp.int32, (HG_BLOCK, HG_WIDTH), 0)
        per_chunk = HG_CHUNK // HG_BLOCK

        def hg_step(i, carry):
            r0 = pl.multiple_of(i * HG_BLOCK, HG_BLOCK)
            qb = qf_s[pl.ds(r0, HG_BLOCK), :]
            bb = b_s[pl.ds(r0, HG_BLOCK), :]
            kb = kk_s[pl.ds(r0, HG_BLOCK), :]
            b_prev = jnp.where(i % per_chunk == 0, 0.0, b_s[pl.ds(jnp.maximum(r0 - 1, 0), 1), :])
            b_end = b_s[pl.ds(r0 + HG_BLOCK - 1, 1), :]
            pieces = []
            for j in range(HG_BLOCK):
                bj = b_s[pl.ds(r0 + j, 1), :]
                kj_row = kk_s[pl.ds(r0 + j, 1), :]
                d = jnp.where(rows16 >= j, (qb * kj_row) * jnp.exp(bb - bj), 0.0)
                pieces.append(d.astype(BF16))
            dall = jnp.concatenate(pieces, axis=0)
            sall = _dot(dall, eblk)
            acc = jnp.zeros((HG_BLOCK, HG_WIDTH), F32)
            for j in range(HG_BLOCK):
                vj = vi_s[pl.ds(r0 + j, 1), :]
                acc = acc + sall[j * HG_BLOCK:(j + 1) * HG_BLOCK, :] * vj
            st_old = st_ref[...]
            qs_b = (qb * jnp.exp(bb - b_prev)).astype(BF16)
            ke_b = (kb * jnp.exp(b_end - bb)).astype(BF16)
            oa_s[pl.ds(r0, HG_BLOCK), :] = acc + _dot_nt(qs_b, st_old.astype(BF16))
            vb_b = vi_s[pl.ds(r0, HG_BLOCK), :].astype(BF16)
            st_ref[...] = st_old * jnp.exp(b_end - b_prev) + jnp.where(emask, _dot_tn(vb_b, ke_b), 0.0)
            return carry

        lax.fori_loop(0, tile // HG_BLOCK, hg_step, 0)

    b2 = b_loc * LOG2E
    n_sub = HG_CHUNK // sub
    half = HG_CHUNK // 2
    lane_head = lax.broadcasted_iota(jnp.int32, (HG_CHUNK, HG_WIDTH), 1) // HG_DK
    head_keep = [jnp.where(lane_head == hd, 1.0, 0.0).astype(BF16) for hd in range(HG_HEADS)]

    def stack_heads(a):
        ab = a.astype(BF16)
        return jnp.concatenate([ab * keep for keep in head_keep], axis=0)

    ti = lax.broadcasted_iota(jnp.int32, (HG_HEADS * HG_CHUNK, HG_CHUNK), 0) % HG_CHUNK
    tj = lax.broadcasted_iota(jnp.int32, (HG_HEADS * HG_CHUNK, HG_CHUNK), 1)
    m_same32 = (ti // sub == tj // sub) & (ti >= tj)
    m_same64 = (ti // (2 * sub) == tj // (2 * sub)) & (ti // sub > tj // sub)
    zsub = jnp.zeros((sub, HG_WIDTH), F32)
    zhalf = jnp.zeros((half, HG_WIDTH), F32)
    def hg_chunk(c, st):
        r = slice(c * HG_CHUNK, (c + 1) * HG_CHUNK)
        bl = b2[r]
        qc = qf[r]
        kc = kk[r]
        vc = vi[r]
        blk = lambda a, i: a[i * sub:(i + 1) * sub]
        row = lambda i: bl[i:i + 1]
        mids = [row(i * sub + sub // 2 - 1) for i in range(n_sub)]
        q_a = jnp.concatenate([blk(qc, i) * jnp.exp2(blk(bl, i) - mids[i]) for i in range(n_sub)], axis=0)
        k_a = jnp.concatenate([blk(kc, i) * jnp.exp2(mids[i] - blk(bl, i)) for i in range(n_sub)], axis=0)
        q_b = jnp.concatenate(
            [zsub if i % 2 == 0 else blk(qc, i) * jnp.exp2(blk(bl, i) - row(i * sub - 1)) for i in range(n_sub)],
            axis=0)
        k_b = jnp.concatenate(
            [blk(kc, i) * jnp.exp2(row((i + 1) * sub - 1) - blk(bl, i)) if i % 2 == 0 else zsub
             for i in range(n_sub)], axis=0)
        q_c = jnp.concatenate([zhalf, qc[half:] * jnp.exp2(bl[half:] - row(half - 1))], axis=0)
        k_c = jnp.concatenate([kc[:half] * jnp.exp2(row(half - 1) - bl[:half]), zhalf], axis=0)
        s_a = _dot_nt(stack_heads(q_a), k_a.astype(BF16))
        s_b = _dot_nt(stack_heads(q_b), k_b.astype(BF16))
        s_c = _dot_nt(stack_heads(q_c), k_c.astype(BF16))
        p = jnp.where(m_same32, s_a, jnp.where(m_same64, s_b, s_c)).astype(BF16)
        pcat = jnp.concatenate([p[hd * HG_CHUNK:(hd + 1) * HG_CHUNK] for hd in range(HG_HEADS)], axis=1)
        o_intra = _dot(pcat, stack_heads(vc))
        b_end = row(HG_CHUNK - 1)
        qs = (qc * jnp.exp2(bl)).astype(BF16)
        ke = (kc * jnp.exp2(b_end - bl)).astype(BF16)
        o_chunk = o_intra + _dot_nt(qs, st.astype(BF16))
        return o_chunk, st * jnp.exp2(b_end) + jnp.where(emask, _dot_tn(vc.astype(BF16), ke), 0.0)

    def pool_group():
        pv = proj[:, OFF_PV:OFF_PV + POOL_WIDTH]
        ptail_ref[POOL_TAIL:POOL_TAIL + tile, :] = pv
        lane_p = lax.broadcasted_iota(jnp.int32, (1, POOL_WIDTH), 1)
        win_l = jnp.left_shift(2, lane_p // POOL_GDIM)
        groups_per_col = 128 // POOL_GDIM
        first_group = lax.broadcasted_iota(jnp.int32, (tile, 128), 1) < POOL_GDIM
        cols = []
        for col in range(POOL_WIDTH // 128):
            cur = ptail_ref[:, col * 128:(col + 1) * 128]
            start = 0
            sums = []
            for level in range((col + 1) * groups_per_col):
                shift = 1 << level
                rows = cur.shape[0]
                cur = cur[SUBLANES:] + cur[SUBLANES - shift:rows - shift]
                start += SUBLANES
                sums.append(cur[POOL_TAIL - start:])
            cols.append(jnp.where(first_group, sums[-2], sums[-1]))
        acc_p = jnp.concatenate(cols, axis=1)
        t_glob = s_idx * tile + lax.broadcasted_iota(jnp.int32, (tile, POOL_WIDTH), 0)
        cnt = jnp.minimum(t_glob + 1, win_l).astype(F32)
        pooled = acc_p / cnt - pv
        ptail_ref[0:POOL_TAIL, :] = ptail_ref[pl.ds(tile, POOL_TAIL), :]
        return _dot(pooled.astype(BF16), poolw_ref[...]) * pools_ref[...]

    o_b = pool_group()

    group = ATT_HEADS // ATT_KV_HEADS
    pair_rows = (group // 2) * ATT_BLOCK
    low_half = lax.broadcasted_iota(jnp.int32, (ATT_BLOCK, ATT_KV_WIDTH), 1) < ATT_HDIM
    qi = lax.broadcasted_iota(jnp.int32, (pair_rows, ATT_BLOCK), 0) % ATT_BLOCK
    kj = lax.broadcasted_iota(jnp.int32, (pair_rows, ATT_BLOCK), 1)
    upper = kj > qi
    first_pair = lax.broadcasted_iota(jnp.int32, (pair_rows, 1), 0) < ATT_BLOCK
    q_scale = (ATT_HDIM ** -0.5) * LOG2E
    def att_block(a, prev):
        r = slice(a * ATT_BLOCK, (a + 1) * ATT_BLOCK)
        cur_k = _kv_variants(proj[r, OFF_AK:OFF_AK + ATT_KV_WIDTH], low_half)
        cur_v = _kv_variants(proj[r, OFF_AV:OFF_AV + ATT_KV_WIDTH], low_half)
        if prev is None:
            prev_k = tuple(kv_ref[n] for n in range(4))
            prev_v = tuple(kv_ref[4 + n] for n in range(4))
            no_prev = jnp.where(upper & (s_idx == 0), -jnp.inf, 0.0)
        else:
            prev_k, prev_v = prev
            no_prev = None
        pair_out = []
        for g in range(ATT_KV_HEADS):
            c0 = OFF_AQ + g * group * ATT_HDIM
            q2 = jnp.concatenate(
                [(proj[r, c0 + p * 2 * ATT_HDIM:c0 + (p + 1) * 2 * ATT_HDIM] * q_scale).astype(BF16)
                 for p in range(group // 2)], axis=0)
            kcat = jnp.concatenate([prev_k[2 * g], cur_k[2 * g], prev_k[2 * g + 1], cur_k[2 * g + 1]], axis=0)
            s_all = _dot_nt(q2, kcat)
            o_g = None
            for half in range(2):
                s_prev = s_all[:, (2 * half) * ATT_BLOCK:(2 * half + 1) * ATT_BLOCK]
                s_cur = s_all[:, (2 * half + 1) * ATT_BLOCK:(2 * half + 2) * ATT_BLOCK]
                if no_prev is not None:
                    s_prev = s_prev + no_prev
                s = jnp.where(upper, s_prev, s_cur)
                head0 = g * group + half
                sink = jnp.where(first_pair, sink_ref[head0], sink_ref[head0 + 2]) * LOG2E
                m = jnp.maximum(jnp.max(s, axis=-1, keepdims=True), sink)
                pexp = jnp.exp2(s - m)
                denom = jnp.sum(pexp, axis=-1, keepdims=True) + jnp.exp2(sink - m)
                pcat = jnp.concatenate([jnp.where(upper, pexp, 0.0).astype(BF16),
                                        jnp.where(upper, 0.0, pexp).astype(BF16)], axis=1)
                vcat = jnp.concatenate([prev_v[2 * g + half], cur_v[2 * g + half]], axis=0)
                o_h = _dot(pcat, vcat) * (1.0 / denom)
                o_g = o_h if o_g is None else o_g + o_h
            for p in range(group // 2):
                pair_out.append(o_g[p * ATT_BLOCK:(p + 1) * ATT_BLOCK])
        return jnp.concatenate(pair_out, axis=1), (cur_k, cur_v)

    assert HG_CHUNK == ATT_BLOCK
    st = st_ref[...]
    prev = None
    oa_chunks, o_c_blocks = [], []
    for j in range(tile // ATT_BLOCK):
        o_chunk, st = hg_chunk(j, st)
        oa_chunks.append(o_chunk)
        o_block, prev = att_block(j, prev)
        o_c_blocks.append(o_block)
    oa_fast = jnp.concatenate(oa_chunks, axis=0)
    st_fast = st
    for n in range(4):
        kv_ref[n] = prev[0][n]
        kv_ref[4 + n] = prev[1][n]
    o_c = jnp.concatenate(o_c_blocks, axis=0)

    mix_bc = _dot(jnp.concatenate([o_b.astype(BF16), o_c.astype(BF16)], axis=1), wout_ref[0, HG_WIDTH:, :])

    oa_s[...] = oa_fast

    @pl.when(factorisable)
    def _():
        st_ref[...] = st_fast

    pl.when(jnp.logical_not(factorisable))(pairwise_path)

    oa = oa_s[...]
    ms_h = _dot((oa * oa).astype(BF16), eblk) * (1.0 / HG_DV)
    o_a = oa * lax.rsqrt(ms_h + EPS) * out_gain

    mix = mix_bc + _dot(o_a.astype(BF16), wout_ref[0, 0:HG_WIDTH, :])
    msm = jnp.mean(mix * mix, axis=-1, keepdims=True)
    o_ref[0] = x_ref[0] + (mix * lax.rsqrt(msm + EPS)) * (gt1 * gpost_ref[...])


def _const_spec(shape):
    nd = len(shape)
    return pl.BlockSpec(shape, lambda *_, _nd=nd: (0,) * _nd, pipeline_mode=pl.Buffered(1))


def _layer_spec(shape, layer):
    nd = len(shape)
    return pl.BlockSpec((1,) + tuple(shape[1:]), lambda *_, _nd=nd, _l=layer: (_l,) + (0,) * (_nd - 1),
                        pipeline_mode=pl.Buffered(1))


def _mix_call(layer, x, mod, g_pre, g_post, w_in, lbp, hg_norm_t, pool_bd, pool_scale, sinks, w_out, eblk, cum):
    bsz, seq, d = x.shape
    tile = min(MIX_TILE, seq)
    grid = (bsz, seq // tile)
    row = lambda v: v.reshape(1, -1)
    in_specs = [
        pl.BlockSpec((1, tile, d), lambda b, s: (b, s, 0)),
        pl.BlockSpec((1, 6, d), lambda b, s: (b, 0, 0)),
        _const_spec((1, d)), _const_spec((1, d)),
        _layer_spec(w_in.shape, layer),
        _layer_spec(lbp.shape, layer),
        _const_spec((1, HG_WIDTH)),
        _const_spec(pool_bd.shape),
        _const_spec((1, POOL_WIDTH)),
        pl.BlockSpec(memory_space=pltpu.SMEM),
        _layer_spec(w_out.shape, layer),
        _const_spec(eblk.shape),
        _const_spec(cum.shape),
    ]
    scratch = [
        pltpu.VMEM((1,) + tuple(w_in.shape[1:]), BF16),
        pltpu.VMEM((1,) + tuple(w_out.shape[1:]), BF16),
        pltpu.VMEM((HG_WIDTH, HG_WIDTH), F32),
        pltpu.VMEM((8, ATT_BLOCK, ATT_KV_WIDTH), BF16),
        pltpu.VMEM((POOL_TAIL + tile, POOL_WIDTH), F32),
        pltpu.VMEM((tile, HG_WIDTH), F32),
        pltpu.VMEM((tile, HG_WIDTH), F32),
        pltpu.VMEM((tile, HG_WIDTH), F32),
        pltpu.VMEM((tile, HG_WIDTH), F32),
        pltpu.VMEM((tile, HG_WIDTH), F32),
    ]
    return pl.pallas_call(
        _mix_kernel,
        grid=grid,
        in_specs=in_specs,
        out_specs=pl.BlockSpec((1, tile, d), lambda b, s: (b, s, 0)),
        out_shape=jax.ShapeDtypeStruct(x.shape, F32),
        scratch_shapes=scratch,
        compiler_params=pltpu.CompilerParams(
            dimension_semantics=("arbitrary", "arbitrary"), vmem_limit_bytes=VMEM_LIMIT),
        name="token_mixer",
    )(x, mod, row(g_pre), row(g_post), w_in, lbp, row(hg_norm_t), pool_bd, row(pool_scale), sinks,
      w_out, eblk, cum)


def _gelu_tanh(x):
    c = 0.7978845608028654
    return (0.5 * x) * (1.0 + jnp.tanh(x * (c + (c * 0.044715) * (x * x))))


def _load_weights_bf16(src_hbm, layer, dst_ref, stage_ref, sem_ref, axis):
    slots = stage_ref.shape[0]
    step = stage_ref.shape[1 + axis]
    n = src_hbm.shape[1 + axis] // step

    def window(ref, j):
        return ref.at[:, pl.ds(j * step, step)] if axis == 1 else ref.at[pl.ds(j * step, step), :]

    def copy(j):
        return pltpu.make_async_copy(window(src_hbm.at[layer], j), stage_ref.at[j % slots], sem_ref.at[j % slots])

    for j in range(min(slots - 1, n)):
        copy(j).start()
    for j in range(n):
        if j + slots - 1 < n:
            copy(j + slots - 1).start()
        copy(j).wait()
        if axis == 1:
            dst_ref[0, :, j * step:(j + 1) * step] = stage_ref[j % slots].astype(BF16)
        else:
            dst_ref[0, j * step:(j + 1) * step, :] = stage_ref[j % slots].astype(BF16)


def _ffn_kernel(layer, x_ref, mod_ref, gpre_ref, gpost_ref, wup_hbm, cw_ref, cb_ref, wdown_hbm,
                o_ref,
                wup_ref, wdown_ref, stage_up, stage_down, sem_up, sem_down, tail_ref, ubuf_ref, act_ref):
    s_idx = pl.program_id(1)
    tile = x_ref.shape[1]
    d_model = x_ref.shape[2]

    @pl.when((pl.program_id(0) == 0) & (s_idx == 0))
    def _():
        _load_weights_bf16(wup_hbm, layer, wup_ref, stage_up, sem_up, axis=1)
        _load_weights_bf16(wdown_hbm, layer, wdown_ref, stage_down, sem_down, axis=0)

    @pl.when(s_idx == 0)
    def _():
        tail_ref[...] = jnp.zeros_like(tail_ref)

    x = x_ref[0]
    ms = jnp.mean(x * x, axis=-1, keepdims=True)
    gain = gpre_ref[...] * (1.0 + mod_ref[0, 4:5, :])
    hb = ((x * lax.rsqrt(ms + EPS)) * gain + mod_ref[0, 3:4, :]).astype(BF16)

    def up_cols(off, slot):
        ubuf = ubuf_ref.at[slot]
        u = _dot(hb, wup_ref[0, :, off:off + FFN_CHUNK])
        ubuf[0:SUBLANES, :] = tail_ref[:, off:off + FFN_CHUNK]
        ubuf[SUBLANES:SUBLANES + tile, :] = u
        tail_ref[:, off:off + FFN_CHUNK] = ubuf[pl.ds(tile, SUBLANES), :]
        return u

    def conv_cols(u, off, slot):
        ubuf = ubuf_ref.at[slot]
        w0 = cw_ref[0:1, off:off + FFN_CHUNK]
        w1 = cw_ref[1:2, off:off + FFN_CHUNK]
        w2 = cw_ref[2:3, off:off + FFN_CHUNK]
        return (w0 * ubuf[pl.ds(SUBLANES - 2, tile), :] + w1 * ubuf[pl.ds(SUBLANES - 1, tile), :]
                + w2 * u + cb_ref[0:1, off:off + FFN_CHUNK])

    n_chunks = D_FF // FFN_CHUNK
    pending = (up_cols(0, 0), up_cols(D_FF, 1))
    for c in range(n_chunks):
        u_gate, u_val = pending
        if c + 1 < n_chunks:
            nxt = 2 * ((c + 1) % 2)
            pending = (up_cols((c + 1) * FFN_CHUNK, nxt), up_cols(D_FF + (c + 1) * FFN_CHUNK, nxt + 1))
        if c == n_chunks - FFN_EARLY_DOWN:
            k_early = c * FFN_CHUNK
            ys_early = [_dot(act_ref[:, 0:k_early], wdown_ref[0, 0:k_early, n:n + FFN_CHUNK])
                        for n in range(0, d_model, FFN_CHUNK)]
        slot = 2 * (c % 2)
        g = conv_cols(u_gate, c * FFN_CHUNK, slot)
        v = conv_cols(u_val, D_FF + c * FFN_CHUNK, slot + 1)
        act_ref[:, c * FFN_CHUNK:(c + 1) * FFN_CHUNK] = _gelu_tanh(g.astype(BF16)) * v.astype(BF16)

    ys = [y_early + _dot(act_ref[:, k_early:], wdown_ref[0, k_early:, n:n + FFN_CHUNK])
          for y_early, n in zip(ys_early, range(0, d_model, FFN_CHUNK))]
    ssq = None
    for y in ys:
        part = jnp.sum(y * y, axis=-1, keepdims=True)
        ssq = part if ssq is None else ssq + part
    rs = lax.rsqrt(ssq * (1.0 / d_model) + EPS)
    out_gain = mod_ref[0, 5:6, :] * gpost_ref[...]
    for g, y in enumerate(ys):
        cols = slice(g * FFN_CHUNK, (g + 1) * FFN_CHUNK)
        o_ref[0, :, cols] = x_ref[0, :, cols] + (y * rs) * out_gain[:, cols]


def _ffn_call(layer, x, mod, g_pre, g_post, w_up, conv_w, conv_b, w_down):
    bsz, seq, d = x.shape
    tile = min(FFN_TILE, seq)
    grid = (bsz, seq // tile)
    row = lambda v: v.reshape(1, -1)
    in_specs = [
        pl.BlockSpec((1, tile, d), lambda b, s: (b, s, 0)),
        pl.BlockSpec((1, 6, d), lambda b, s: (b, 0, 0)),
        _const_spec((1, d)), _const_spec((1, d)),
        pl.BlockSpec(memory_space=pl.ANY),
        _const_spec(conv_w.shape),
        _const_spec((1, 2 * D_FF)),
        pl.BlockSpec(memory_space=pl.ANY),
    ]
    scratch = [
        pltpu.VMEM((1,) + tuple(w_up.shape[1:]), BF16),
        pltpu.VMEM((1,) + tuple(w_down.shape[1:]), BF16),
        pltpu.VMEM((FFN_STAGE_SLOTS, w_up.shape[1], FFN_STAGE), F32),
        pltpu.VMEM((FFN_STAGE_SLOTS, FFN_STAGE // 2, w_down.shape[2]), F32),
        pltpu.SemaphoreType.DMA((FFN_STAGE_SLOTS,)),
        pltpu.SemaphoreType.DMA((FFN_STAGE_SLOTS,)),
        pltpu.VMEM((SUBLANES, 2 * D_FF), F32),
        pltpu.VMEM((4, SUBLANES + tile, FFN_CHUNK), F32),
        pltpu.VMEM((tile, D_FF), BF16),
    ]
    return pl.pallas_call(
        functools.partial(_ffn_kernel, layer),
        grid=grid,
        in_specs=in_specs,
        out_specs=pl.BlockSpec((1, tile, d), lambda b, s: (b, s, 0)),
        out_shape=jax.ShapeDtypeStruct(x.shape, F32),
        scratch_shapes=scratch,
        compiler_params=pltpu.CompilerParams(
            dimension_semantics=("arbitrary", "arbitrary"), vmem_limit_bytes=VMEM_LIMIT),
        name="conv_ffn",
    )(x, mod, row(g_pre), row(g_post), w_up, conv_w, row(conv_b), w_down)


def _block_diag(blocks):
    g, c, _ = blocks.shape
    eye = jnp.eye(g, dtype=blocks.dtype)
    return (eye[:, None, :, None] * blocks[:, :, None, :]).reshape(g * c, g * c)


def _cum_matrix(rows):
    r = jnp.arange(rows)
    return (r[None, :] <= r[:, None]).astype(BF16)


def kernel(x, c, w_ada, b_ada, g_pre_mix, g_post_mix, w_in, hg_lb_logits, hg_norm, pool_w, pool_scale,
           attn_sinks, w_out, g_pre_ffn, g_post_ffn, w_up, conv_w, conv_b, w_down):
    depth = w_ada.shape[0]
    bsz, seq, d = x.shape
    mod_all, lbp_all = _ada_call(c, w_ada, b_ada, hg_lb_logits)
    mod_all = mod_all.reshape(depth, bsz, 6, d)
    eblk = _block_diag(jnp.ones((HG_HEADS, HG_DK, HG_DV), BF16))
    cum = _cum_matrix(HG_CHUNK)
    for l in range(depth):
        x = _mix_call(
            l, x, mod_all[l], g_pre_mix[l], g_post_mix[l], w_in, lbp_all,
            jnp.tile(hg_norm[l], HG_HEADS), _block_diag(pool_w[l]).astype(BF16), pool_scale[l],
            attn_sinks[l], w_out, eblk, cum)
        x = _ffn_call(
            l, x, mod_all[l], g_pre_ffn[l], g_post_ffn[l], w_up, conv_w[l], conv_b[l], w_down)
    return x
```
